```python
import jax, jax.numpy as jnp
from jax import lax
import numpy as np

D_MODEL = 1024
BATCH = 8
SEQ = 4096
DEPTH = 1

DN_HEADS = 8
DN_HEAD_DIM = 64
DN_WIDTH = DN_HEADS * DN_HEAD_DIM
DN_CONV = 3
DN_CHUNK = 64
ATT_GROUPS = ((128, 1), (512, 4), (2048, 16))
N_GROUPS = 3
ATT_HEADS_PER_GROUP = 8
ATT_HEAD_DIM = 64
ATT_HEADS = N_GROUPS * ATT_HEADS_PER_GROUP
ATT_WIDTH = ATT_HEADS * ATT_HEAD_DIM
ATT_OUT_WIDTH = ATT_HEADS_PER_GROUP * ATT_HEAD_DIM
ATT_BLOCK = 64
ALIBI_MAX = 8.0
D_FF = 2816
FFN_CONV = 3
IN_PROJ_SIZES = (3 * DN_WIDTH, DN_WIDTH, 2 * DN_HEADS, 2 * DN_HEADS,
                 ATT_WIDTH, ATT_WIDTH, ATT_WIDTH, D_MODEL, D_MODEL)
D_IN_PROJ = 3 * DN_WIDTH + DN_WIDTH + 4 * DN_HEADS + 3 * ATT_WIDTH + 2 * D_MODEL
EPS = 1e-6
NEG_INF = -1e30

kernel_name = "hybrid_deltanet_dilated_attn_convffn_block"


def rms_norm(x, w):
    xf = x.astype(jnp.float32)
    y = xf * lax.rsqrt(jnp.mean(xf * xf, axis=-1, keepdims=True) + EPS)
    return (y * w.astype(jnp.float32)).astype(x.dtype)


def l2_normalize(t):
    tf = t.astype(jnp.float32)
    return tf * lax.rsqrt(jnp.sum(tf * tf, axis=-1, keepdims=True) + EPS)


def depthwise_conv_centred(x, w):
    width, c = w.shape
    return lax.conv_general_dilated(
        x, w[:, None, :].astype(x.dtype), window_strides=(1,),
        padding=[((width - 1) // 2, width // 2)],
        dimension_numbers=('NWC', 'WIO', 'NWC'), feature_group_count=c)


def alibi_slopes(n):
    return 2.0 ** (-ALIBI_MAX * jnp.arange(1, n + 1, dtype=jnp.float32) / n)


def gated_delta_rule_chunked(q, k, v, g, beta):
    f32 = jnp.float32
    n, h, t, dk = q.shape
    dv = v.shape[-1]
    c = DN_CHUNK
    nc = t // c
    q = q.astype(f32).reshape(n, h, nc, c, dk) * (dk ** -0.5)
    k = k.astype(f32).reshape(n, h, nc, c, dk)
    v = v.astype(f32).reshape(n, h, nc, c, dv)
    beta = beta.astype(f32).reshape(n, h, nc, c)
    g = jnp.cumsum(g.astype(f32).reshape(n, h, nc, c), axis=-1)
    idx = jnp.arange(c)
    incl = idx[:, None] >= idx[None, :]
    strict = idx[:, None] > idx[None, :]
    decay = jnp.exp(jnp.where(incl, g[..., :, None] - g[..., None, :], -jnp.inf))
    kb = k * beta[..., None]
    a_low = jnp.where(strict, jnp.einsum('nhcid,nhcjd->nhcij', kb, k) * decay, 0.0)
    eye = jnp.eye(c, dtype=f32)
    tmat = lax.linalg.triangular_solve(eye + a_low, jnp.broadcast_to(eye, a_low.shape),
                                       left_side=True, lower=True, unit_diagonal=True)
    u = jnp.einsum('nhcij,nhcjv->nhciv', tmat, v * beta[..., None])
    w = jnp.einsum('nhcij,nhcjk->nhcik', tmat, kb * jnp.exp(g)[..., None])
    qk = jnp.einsum('nhcid,nhcjd->nhcij', q, k) * decay
    g_last = g[..., -1]
    q_dec = q * jnp.exp(g)[..., None]
    k_dec = k * jnp.exp(g_last[..., None] - g)[..., None]
    xs = tuple(jnp.moveaxis(arr, 2, 0) for arr in (u, w, qk, q_dec, k_dec, g_last))

    def step(state, inp):
        u_i, w_i, qk_i, qd_i, kd_i, gl_i = inp
        v_new = u_i - jnp.einsum('nhck,nhkv->nhcv', w_i, state)
        o_i = jnp.einsum('nhck,nhkv->nhcv', qd_i, state) + jnp.einsum('nhij,nhjv->nhiv', qk_i, v_new)
        state = state * jnp.exp(gl_i)[..., None, None] + jnp.einsum('nhck,nhcv->nhkv', kd_i, v_new)
        return state, o_i

    _, o = lax.scan(step, jnp.zeros((n, h, dk, dv), f32), xs)
    return jnp.moveaxis(o, 0, 2).reshape(n, h, t, dv)


def dilated_window_attention(q, k, v, slopes, dilation, radius):
    b, s, h, dh = q.shape
    blk = ATT_BLOCK
    sub = s // dilation
    nb = -(-sub // blk)
    lp = nb * blk

    def to_sub(a):
        return a.reshape(b, sub, dilation, h, dh).transpose(0, 2, 3, 1, 4)

    qs = jnp.pad(to_sub(q), ((0, 0),) * 3 + ((0, lp - sub), (0, 0))).reshape(b, dilation, h, nb, blk, dh)

    def windows(a):
        ap = jnp.pad(to_sub(a), ((0, 0),) * 3 + ((blk, lp - sub + blk), (0, 0)))
        ap = ap.reshape(b, dilation, h, nb + 2, blk, dh)
        return jnp.concatenate([ap[:, :, :, 0:nb], ap[:, :, :, 1:nb + 1], ap[:, :, :, 2:nb + 2]], axis=-2)

    kw, vw = windows(k), windows(v)
    blocks = jnp.arange(nb)[:, None, None]
    qi = blocks * blk + jnp.arange(blk)[None, :, None]
    kj = (blocks - 1) * blk + jnp.arange(3 * blk)[None, None, :]
    rel = jnp.abs(qi - kj)
    valid = (rel <= radius) & (kj >= 0) & (kj < sub)
    scores = jnp.einsum('bdhnqe,bdhnke->bdhnqk', qs, kw) * (dh ** -0.5)
    scores = scores - slopes[:, None, None, None] * (rel * dilation).astype(jnp.float32)
    scores = jnp.where(valid, scores, NEG_INF)
    m = jnp.max(scores, axis=-1, keepdims=True)
    lse = m + jnp.log(jnp.sum(jnp.exp(scores - m), axis=-1, keepdims=True))
    o = jnp.einsum('bdhnqk,bdhnke->bdhnqe', jnp.exp(scores - lse), vw)
    o = o.reshape(b, dilation, h, lp, dh)[:, :, :, :sub].transpose(0, 3, 1, 2, 4).reshape(b, s, h, dh)
    lse = lse.reshape(b, dilation, h, lp)[..., :sub].transpose(0, 3, 1, 2).reshape(b, s, h)
    return o, lse


def hybrid_block(x, norm1_w, w_in, dn_conv_w, dn_a_log, dn_dt_bias, dn_out_norm_w,
                 attn_q_norm_w, attn_k_norm_w, w_dn_out, w_attn_out, w_o,
                 norm2_w, w_ffn_up, ffn_conv_w, w_ffn_down):
    f32 = jnp.float32
    b, s, _ = x.shape
    h = rms_norm(x, norm1_w)
    proj = h @ w_in
    points = np.cumsum(np.array(IN_PROJ_SIZES))[:-1].tolist()
    qkv_a, z_a, a_in, b_in, q_b, k_b, v_b, gate_a, gate_b = jnp.split(proj, points, axis=-1)

    qkv_a = jax.nn.silu(depthwise_conv_centred(qkv_a, dn_conv_w))
    q_a, k_a, v_a = [t.reshape(b, s, DN_HEADS, DN_HEAD_DIM).transpose(0, 2, 1, 3)
                     for t in jnp.split(qkv_a, 3, axis=-1)]
    q_a, k_a = l2_normalize(q_a), l2_normalize(k_a)
    a_dir = a_in.astype(f32).reshape(b, s, 2, DN_HEADS)
    beta = jax.nn.sigmoid(b_in.astype(f32).reshape(b, s, 2, DN_HEADS))
    g = -jnp.exp(dn_a_log.astype(f32)) * jax.nn.softplus(a_dir + dn_dt_bias.astype(f32))
    g = g.transpose(2, 0, 3, 1)
    beta = beta.transpose(2, 0, 3, 1)
    both = lambda t: jnp.concatenate([t, jnp.flip(t, axis=2)], axis=0)
    o2 = gated_delta_rule_chunked(both(q_a), both(k_a), both(v_a),
                                  jnp.concatenate([g[0], jnp.flip(g[1], axis=2)], axis=0),
                                  jnp.concatenate([beta[0], jnp.flip(beta[1], axis=2)], axis=0))
    o_a = (o2[:b] + jnp.flip(o2[b:], axis=2)).transpose(0, 2, 1, 3)
    z = z_a.astype(f32).reshape(b, s, DN_HEADS, DN_HEAD_DIM)
    o_a = rms_norm(o_a, dn_out_norm_w) * jax.nn.silu(z)
    y_a = o_a.reshape(b, s, DN_WIDTH).astype(x.dtype) @ w_dn_out

    heads = lambda t: t.astype(f32).reshape(b, s, N_GROUPS, ATT_HEADS_PER_GROUP, ATT_HEAD_DIM)
    q_b = rms_norm(heads(q_b), attn_q_norm_w)
    k_b = rms_norm(heads(k_b), attn_k_norm_w)
    v_b = heads(v_b)
    slopes = alibi_slopes(ATT_HEADS).reshape(N_GROUPS, ATT_HEADS_PER_GROUP)
    outs, lses = [], []
    for gi, (window, dil) in enumerate(ATT_GROUPS):
        o_g, l_g = dilated_window_attention(q_b[:, :, gi], k_b[:, :, gi], v_b[:, :, gi],
                                            slopes[gi], dil, window // (2 * dil))
        outs.append(o_g)
        lses.append(l_g)
    wts = jax.nn.softmax(jnp.stack(lses), axis=0)
    o_b = jnp.einsum('gbsh,gbshe->bshe', wts, jnp.stack(outs))
    y_b = o_b.reshape(b, s, ATT_OUT_WIDTH).astype(x.dtype) @ w_attn_out

    mixed = jax.nn.sigmoid(gate_a) * y_a + jax.nn.sigmoid(gate_b) * y_b
    x = x + mixed @ w_o

    h2 = rms_norm(x, norm2_w)
    gate, up = jnp.split(depthwise_conv_centred(h2 @ w_ffn_up, ffn_conv_w), 2, axis=-1)
    return x + (jax.nn.silu(gate) * up) @ w_ffn_down


def setup_inputs(seed: int = 0) -> dict:
    key = jax.random.key(seed)
    ks = jax.random.split(key, 17)
    f32 = jnp.float32
    normal = lambda k, shape, scale: jax.random.normal(k, shape, f32) * scale
    dt = jnp.exp(jax.random.uniform(ks[5], (DEPTH, 2, DN_HEADS), f32, np.log(1e-3), np.log(1e-1)))
    return {
        "x": normal(ks[0], (BATCH, SEQ, D_MODEL), 1.0),
        "norm1_w": 1.0 + normal(ks[1], (DEPTH, D_MODEL), 0.05),
        "w_in": normal(ks[2], (DEPTH, D_MODEL, D_IN_PROJ), D_MODEL ** -0.5),
        "dn_conv_w": normal(ks[3], (DEPTH, DN_CONV, 3 * DN_WIDTH), DN_CONV ** -0.5),
        "dn_a_log": jnp.log(jax.random.uniform(ks[4], (DEPTH, 2, DN_HEADS), f32, 1.0, 16.0)),
        "dn_dt_bias": dt + jnp.log(-jnp.expm1(-dt)),
        "dn_out_norm_w": 1.0 + normal(ks[6], (DEPTH, DN_HEAD_DIM), 0.05),
        "attn_q_norm_w": 1.0 + normal(ks[7], (DEPTH, ATT_HEAD_DIM), 0.05),
        "attn_k_norm_w": 1.0 + normal(ks[8], (DEPTH, ATT_HEAD_DIM), 0.05),
        "w_dn_out": normal(ks[9], (DEPTH, DN_WIDTH, D_MODEL), DN_WIDTH ** -0.5),
        "w_attn_out": normal(ks[10], (DEPTH, ATT_OUT_WIDTH, D_MODEL), ATT_OUT_WIDTH ** -0.5),
        "w_o": normal(ks[11], (DEPTH, D_MODEL, D_MODEL), D_MODEL ** -0.5),
        "norm2_w": 1.0 + normal(ks[12], (DEPTH, D_MODEL), 0.05),
        "w_ffn_up": normal(ks[13], (DEPTH, D_MODEL, 2 * D_FF), D_MODEL ** -0.5),
        "ffn_conv_w": normal(ks[14], (DEPTH, FFN_CONV, 2 * D_FF), FFN_CONV ** -0.5),
        "w_ffn_down": normal(ks[15], (DEPTH, D_FF, D_MODEL), D_FF ** -0.5),
    }


def reference(x, norm1_w, w_in, dn_conv_w, dn_a_log, dn_dt_bias, dn_out_norm_w,
              attn_q_norm_w, attn_k_norm_w, w_dn_out, w_attn_out, w_o,
              norm2_w, w_ffn_up, ffn_conv_w, w_ffn_down):
    for layer in range(DEPTH):
        x = hybrid_block(x, norm1_w[layer], w_in[layer], dn_conv_w[layer], dn_a_log[layer],
                         dn_dt_bias[layer], dn_out_norm_w[layer], attn_q_norm_w[layer],
                         attn_k_norm_w[layer], w_dn_out[layer], w_attn_out[layer], w_o[layer],
                         norm2_w[layer], w_ffn_up[layer], ffn_conv_w[layer], w_ffn_down[layer])
    return x
```

```python
import functools

import jax
import jax.numpy as jnp
from jax import lax
from jax.experimental import pallas as pl
from jax.experimental.pallas import tpu as pltpu

F32 = jnp.float32
BF16 = jnp.bfloat16

EPS = 1e-6
NEG_INF = -1e30

HEAD_DIM = 64
HEADS = 8
WIDTH = HEADS * HEAD_DIM
CHUNK = 64
RADIUS = 64
DILATIONS = (1, 4, 16)
ALIBI_MAX = 8.0
N_ATT_HEADS = 24
QUAD = 4 * HEAD_DIM
BF16_ROWS = 16
LANES = 128
V7X_VMEM_LIMIT = 56 * 1024 * 1024


def _cparams(sem):
    return pltpu.CompilerParams(dimension_semantics=sem, vmem_limit_bytes=V7X_VMEM_LIMIT)


def _dot(a, b):
    return jnp.dot(a, b, preferred_element_type=F32)


def _dot_nt(a, b):
    return lax.dot_general(a, b, (((1,), (1,)), ((), ())), preferred_element_type=F32)


def _split3(x):
    hi = x.astype(BF16)
    r1 = x - hi.astype(F32)
    mid = r1.astype(BF16)
    lo = (r1 - mid.astype(F32)).astype(BF16)
    return hi, mid, lo


def _dot3(a, x):
    hi, mid, lo = _split3(x)
    return _dot(a, hi) + _dot(a, mid) + _dot(a, lo)


def _log2(n):
    assert n > 0 and n & (n - 1) == 0, n
    return n.bit_length() - 1


def _div(x, n):
    return lax.shift_right_logical(x, _log2(n))


def _mod(x, n):
    assert n & (n - 1) == 0, n
    return x & (n - 1)


def _iota(shape, dim):
    return lax.broadcasted_iota(jnp.int32, shape, dim)


def _sigmoid(x):
    return 1.0 / (1.0 + jnp.exp(-x))


def _silu(x):
    return x * _sigmoid(x)


def _head_ones(width):
    r = _div(_iota((width, width), 0), HEAD_DIM)
    c = _div(_iota((width, width), 1), HEAD_DIM)
    return (r == c).astype(BF16)


def _head_meansq(x):
    return _dot((x * x).astype(BF16), _head_ones(x.shape[1])) * (1.0 / HEAD_DIM)


def _head_expand():
    r = _iota((LANES, WIDTH), 0)
    c = _div(_iota((LANES, WIDTH), 1), HEAD_DIM)
    return (r == c).astype(BF16)


def _perm_matrix(tm, d, inverse=False):
    run = tm // d
    row = _iota((tm, tm), 0)
    col = _iota((tm, tm), 1)
    if inverse:
        src = _mod(row, d) * run + _div(row, d)
    else:
        src = _mod(row, run) * d + _div(row, run)
    return (col == src).astype(BF16)


def _norm_perm_kernel(x_ref, nw_ref, wab_ref, h_ref, h4_ref, h16_ref, ab_ref):
    x = x_ref[...]
    ms = jnp.mean(x * x, axis=-1, keepdims=True)
    h = (x * lax.rsqrt(ms + EPS) * nw_ref[...]).astype(BF16)
    h_ref[...] = h
    tm = x.shape[0]
    for d, ref in ((4, h4_ref), (16, h16_ref)):
        run = tm // d
        hp = _dot(_perm_matrix(tm, d), h).astype(BF16)
        for r in range(d):
            ref[r] = hp[r * run:(r + 1) * run]
    ab_ref[...] = _dot(h, wab_ref[...])


def _norm_perm(x, norm_w, wab, tm=256):
    b, s, dm = x.shape
    return pl.pallas_call(
        _norm_perm_kernel,
        grid=(b, s // tm),
        in_specs=[
            pl.BlockSpec((None, tm, dm), lambda i, j: (i, j, 0)),
            pl.BlockSpec((1, dm), lambda i, j: (0, 0)),
            pl.BlockSpec((dm, LANES), lambda i, j: (0, 0)),
        ],
        out_specs=[
            pl.BlockSpec((None, tm, dm), lambda i, j: (i, j, 0)),
            pl.BlockSpec((None, 4, tm // 4, dm), lambda i, j: (i, 0, j, 0)),
            pl.BlockSpec((None, 16, tm // 16, dm), lambda i, j: (i, 0, j, 0)),
            pl.BlockSpec((None, tm, LANES), lambda i, j: (i, j, 0)),
        ],
        out_shape=[
            jax.ShapeDtypeStruct((b, s, dm), BF16),
            jax.ShapeDtypeStruct((b, 4, s // 4, dm), BF16),
            jax.ShapeDtypeStruct((b, 16, s // 16, dm), BF16),
            jax.ShapeDtypeStruct((b, s, LANES), F32),
        ],
        compiler_params=_cparams(("parallel", "parallel")),
        name="norm_perm",
    )(x, norm_w, wab)


def _mm_kernel(a_ref, w_ref, o_ref):
    o_ref[...] = _dot(a_ref[...], w_ref[...]).astype(o_ref.dtype)


def _matmul(a, w, out_dtype, name, tm=1024, tn=1024):
    m, k = a.shape
    n = w.shape[1]
    tm = min(tm, m)
    tn = min(tn, n)
    while n % tn:
        tn //= 2
    return pl.pallas_call(
        _mm_kernel,
        grid=(m // tm, n // tn),
        in_specs=[pl.BlockSpec((tm, k), lambda i, j: (i, 0)),
                  pl.BlockSpec((k, tn), lambda i, j: (0, j))],
        out_specs=pl.BlockSpec((tm, tn), lambda i, j: (i, j)),
        out_shape=jax.ShapeDtypeStruct((m, n), out_dtype),
        compiler_params=_cparams(("parallel", "parallel")),
        name=name,
    )(a, w)


def _shifted(x, prev_row, next_row):
    t = x.shape[0]
    row = _iota((t, 1), 0)
    up = jnp.where(row == 0, prev_row, pltpu.roll(x, 1, 0))
    dn = jnp.where(row == t - 1, next_row, pltpu.roll(x, t - 1, 0))
    return up, dn


def _dn_prep_kernel(x_ref, xp_ref, xn_ref, cw_ref, ab_ref, alog_ref, dtb_ref,
                    q_ref, k_ref, v_ref, cp_ref):
    j = pl.program_id(1)
    nj = pl.num_programs(1)
    x = x_ref[...].astype(F32)
    t = x.shape[0]
    prev_row = xp_ref[BF16_ROWS - 1:BF16_ROWS, :].astype(F32) * jnp.where(j > 0, 1.0, 0.0)
    next_row = xn_ref[0:1, :].astype(F32) * jnp.where(j < nj - 1, 1.0, 0.0)
    up, dn = _shifted(x, prev_row, next_row)
    cw = cw_ref[...]
    y = _silu(cw[0:1] * up + cw[1:2] * x + cw[2:3] * dn)
    q = y[:, 0:WIDTH]
    k = y[:, WIDTH:2 * WIDTH]
    q_ref[...] = (q * lax.rsqrt(_head_meansq(q) * HEAD_DIM + EPS)).astype(BF16)
    k_ref[...] = (k * lax.rsqrt(_head_meansq(k) * HEAD_DIM + EPS)).astype(BF16)
    v_ref[...] = y[:, 2 * WIDTH:3 * WIDTH].astype(BF16)

    ab = ab_ref[...]
    a = ab[:, 0:16] + dtb_ref[...]
    softplus = jnp.maximum(a, 0.0) + jnp.log(1.0 + jnp.exp(-jnp.abs(a)))
    g = -jnp.exp(alog_ref[...]) * softplus
    beta = _sigmoid(ab[:, 16:32])
    row = _iota((t, t), 0)
    col = _iota((t, t), 1)
    same = _div(row, CHUNK) == _div(col, CHUNK)
    lower = (same & (row >= col)).astype(BF16)
    upper = (same & (row <= col)).astype(BF16)
    lane = _iota((t, 16), 1)
    gc = jnp.where(lane < HEADS, _dot3(lower, g), _dot3(upper, g))
    prow = _iota((16, LANES), 0)
    pcol = _iota((16, LANES), 1)
    acc = jnp.zeros((t, LANES), F32)
    for idx, piece in enumerate(_split3(gc) + _split3(beta)):
        place = (pcol == prow + 16 * idx).astype(BF16)
        acc = acc + _dot(piece, place)
    cp_ref[...] = acc.astype(BF16)


def _dn_prep(pa, conv_w, ab, alog, dtb, tp=256):
    b, s, _ = pa.shape
    w3 = 3 * WIDTH
    nblk = s // BF16_ROWS
    per = tp // BF16_ROWS
    return pl.pallas_call(
        _dn_prep_kernel,
        grid=(b, s // tp),
        in_specs=[
            pl.BlockSpec((None, tp, w3), lambda i, j: (i, j, 0)),
            pl.BlockSpec((None, BF16_ROWS, w3), lambda i, j: (i, jnp.maximum(j * per - 1, 0), 0)),
            pl.BlockSpec((None, BF16_ROWS, w3), lambda i, j: (i, jnp.minimum((j + 1) * per, nblk - 1), 0)),
            pl.BlockSpec((3, w3), lambda i, j: (0, 0)),
            pl.BlockSpec((None, tp, LANES), lambda i, j: (i, j, 0)),
            pl.BlockSpec((1, 16), lambda i, j: (0, 0)),
            pl.BlockSpec((1, 16), lambda i, j: (0, 0)),
        ],
        out_specs=[pl.BlockSpec((None, tp, WIDTH), lambda i, j: (i, j, 0))] * 3
        + [pl.BlockSpec((None, tp, LANES), lambda i, j: (i, j, 0))],
        out_shape=[jax.ShapeDtypeStruct((b, s, WIDTH), BF16)] * 3
        + [jax.ShapeDtypeStruct((b, s, LANES), BF16)],
        compiler_params=_cparams(("parallel", "parallel")),
        name="dn_prep",
    )(pa, pa, pa, conv_w, ab, alog, dtb)


def _block_diag(x):
    blk = _div(_iota(x.shape, 1), HEAD_DIM)
    parts = [jnp.where(blk == a, x, 0.0) for a in range(4)]
    return jnp.concatenate(parts, axis=0).astype(BF16)


def _dn_scan_kernel(q_ref, k_ref, v_ref, cp_ref, e2_ref, o_ref,
                    s_scr, u_scr, l1_scr, l2_scr, al_scr):
    dirn = pl.program_id(1)
    seg = pl.program_id(2)
    nchunk = q_ref.shape[0] // CHUNK
    fwd = dirn == 0

    @pl.when(seg == 0)
    def _():
        s_scr[...] = jnp.zeros_like(s_scr)

    row = _iota((CHUNK, QUAD), 0)
    col = _mod(_iota((CHUNK, QUAD), 1), HEAD_DIM)
    later = jnp.where(fwd, row, col)
    earlier = jnp.where(fwd, col, row)
    incl = later >= earlier
    strict = later > earlier
    diag = row == col
    eye = diag.astype(BF16)

    def phase_a(c, carry):
        r0 = pl.multiple_of(c * CHUNK, CHUNK)
        rows = pl.ds(r0, CHUNK)
        x = _dot(cp_ref[rows, :], e2_ref[...])
        for qd in range(2):
            lanes = slice(qd * QUAD, (qd + 1) * QUAD)
            gcol = x[:, qd * QUAD:(qd + 1) * QUAD]
            beta = x[:, WIDTH + qd * QUAD:WIDTH + (qd + 1) * QUAD]
            grow = jnp.sum(jnp.where(diag, gcol, 0.0), axis=0, keepdims=True)
            glast = jnp.where(fwd, gcol[CHUNK - 1:CHUNK, :], gcol[0:1, :])
            dm = jnp.exp(jnp.where(incl, gcol - grow, NEG_INF))
            eg = jnp.exp(gcol)
            kq = k_ref[rows, lanes]
            qq = q_ref[rows, lanes]
            kf = kq.astype(F32)
            z = _dot_nt(jnp.concatenate([kq, qq, eye], axis=0), _block_diag(kf))
            a = jnp.where(strict, beta * z[0:CHUNK] * dm, 0.0)
            qkm = jnp.where(incl, z[CHUNK:2 * CHUNK] * dm, 0.0) * 0.125
            kdt = z[2 * CHUNK:3 * CHUNK] * jnp.exp(glast - grow)
            vb = v_ref[rows, lanes].astype(F32) * beta
            kbg = kf * (beta * eg)
            qdec = qq.astype(F32) * (eg * 0.125)
            p = _dot(a.astype(BF16), _block_diag(a))
            n = -a
            for _ in range(4):
                zz = _dot(jnp.concatenate([n, p], axis=0).astype(BF16), _block_diag(p))
                n = n + p + zz[0:CHUNK]
                p = zz[CHUNK:2 * CHUNK]
            n = n + p + _dot(n.astype(BF16), _block_diag(p))
            rhs = jnp.concatenate([_block_diag(vb), _block_diag(kbg)], axis=1)
            uw = _dot(n.astype(BF16), rhs)
            u_scr[qd, c] = vb + uw[:, 0:QUAD]
            w = kbg + uw[:, QUAD:2 * QUAD]
            l1_scr[qd, c] = jnp.concatenate([w, qdec], axis=0).astype(BF16)
            l2_scr[qd, c] = jnp.concatenate([qkm, kdt], axis=0).astype(BF16)
            al_scr[qd, c] = jnp.broadcast_to(jnp.exp(glast), (8, QUAD))
        return carry

    lax.fori_loop(0, nchunk, phase_a, 0)

    def phase_b(i, carry):
        c = jnp.where(fwd, i, nchunk - 1 - i)
        r0 = pl.multiple_of(c * CHUNK, CHUNK)
        new = []
        for qd in range(2):
            s = carry[qd]
            z1 = _dot(l1_scr[qd, c], _block_diag(s))
            vn = u_scr[qd, c] - z1[0:CHUNK]
            z2 = _dot(l2_scr[qd, c], _block_diag(vn))
            o = z1[CHUNK:2 * CHUNK] + z2[0:CHUNK]
            o_ref[pl.ds(r0, CHUNK), qd * QUAD:(qd + 1) * QUAD] = o.astype(o_ref.dtype)
            new.append(al_scr[qd, c][0:1, :] * s + z2[CHUNK:2 * CHUNK])
        return tuple(new)

    s0, s1 = lax.fori_loop(0, nchunk, phase_b, (s_scr[0], s_scr[1]))
    s_scr[0] = s0
    s_scr[1] = s1


def _dn_scan(q, k, v, cp, e2, seg=1024):
    b, s, _ = q.shape
    seg = min(seg, s)
    nseg = s // seg
    nchunk = seg // CHUNK

    def tok(i, d, j):
        return (i, jnp.where(d == 0, j, nseg - 1 - j), 0)

    return pl.pallas_call(
        _dn_scan_kernel,
        grid=(b, 2, nseg),
        in_specs=[pl.BlockSpec((None, seg, WIDTH), tok)] * 3
        + [pl.BlockSpec((None, seg, LANES), tok),
           pl.BlockSpec((None, LANES, 2 * WIDTH), lambda i, d, j: (d, 0, 0))],
        out_specs=pl.BlockSpec((None, None, seg, WIDTH),
                               lambda i, d, j: (d, i, jnp.where(d == 0, j, nseg - 1 - j), 0)),
        out_shape=jax.ShapeDtypeStruct((2, b, s, WIDTH), BF16),
        scratch_shapes=[
            pltpu.VMEM((2, CHUNK, QUAD), F32),
            pltpu.VMEM((2, nchunk, CHUNK, QUAD), F32),
            pltpu.VMEM((2, nchunk, 2 * CHUNK, QUAD), BF16),
            pltpu.VMEM((2, nchunk, 2 * CHUNK, QUAD), BF16),
            pltpu.VMEM((2, nchunk, 8, QUAD), F32),
        ],
        compiler_params=_cparams(("parallel", "arbitrary", "arbitrary")),
        name="dn_scan",
    )(q, k, v, cp, e2)


def _dn_expand_matrix():
    r = jnp.arange(LANES)[:, None]
    c = jnp.arange(2 * WIDTH)[None, :]
    piece, lane = r // 16, r % 16
    out = []
    for d in range(2):
        head = lane - 8 * d
        ok = (piece < 6) & (head >= 0) & (head < HEADS)
        ok = ok & ((c // WIDTH) == (piece // 3)) & (((c % WIDTH) // HEAD_DIM) == head)
        out.append(ok)
    return jnp.stack(out).astype(BF16)


QBLK = 2 * RADIUS


def _attn_kernel(q_ref, k_ref, kp_ref, kn_ref, v_ref, vp_ref, vn_ref, qw_ref, kw_ref,
                 o_ref, lse_ref, *, dil, group, sub):
    t = pl.program_id(1)
    tq = q_ref.shape[0]
    q = q_ref[...].astype(F32)
    qn = (q * lax.rsqrt(_head_meansq(q) + EPS) * (qw_ref[...] * HEAD_DIM ** -0.5)).astype(BF16)
    kx = jnp.concatenate([kp_ref[...], k_ref[...], kn_ref[...]], axis=0).astype(F32)
    kn = (kx * lax.rsqrt(_head_meansq(kx) + EPS) * kw_ref[...]).astype(BF16)
    vx = jnp.concatenate([vp_ref[...], v_ref[...], vn_ref[...]], axis=0)
    a = _iota((QBLK, 2 * QBLK), 0)
    c = _iota((QBLK, 2 * QBLK), 1)
    rel = jnp.abs(a + RADIUS - c)
    base = (rel * (-dil)).astype(F32)
    lane = _iota((1, LANES), 1)
    for qb in range(tq // QBLK):
        kj = t * tq + (qb * QBLK - RADIUS) + c
        valid = (rel <= RADIUS) & (kj >= 0) & (kj < sub)
        rows = slice(qb * QBLK, (qb + 1) * QBLK)
        krows = slice(qb * QBLK, qb * QBLK + 2 * QBLK)
        lse_tile = jnp.zeros((QBLK, LANES), F32)
        for h in range(HEADS):
            slope = 2.0 ** (-ALIBI_MAX * (group * HEADS + h + 1) / N_ATT_HEADS)
            hl = slice(h * HEAD_DIM, (h + 1) * HEAD_DIM)
            sc = _dot_nt(qn[rows, hl], kn[krows, hl]) + slope * base
            sc = jnp.where(valid, sc, NEG_INF)
            m = jnp.max(sc, axis=-1, keepdims=True)
            p = jnp.exp(sc - m)
            l = jnp.sum(p, axis=-1, keepdims=True)
            oh = _dot(p.astype(BF16), vx[krows, hl]) / l
            o_ref[rows, hl] = oh.astype(o_ref.dtype)
            lse_tile = lse_tile + jnp.where(lane == h, m + jnp.log(l), 0.0)
        lse_ref[rows, :] = lse_tile


def _attn(qkv, qw, kw, *, dil, group):
    nseq, sub, _ = qkv.shape
    tq = min(256, sub)
    nt = sub // tq
    per = tq // RADIUS
    nblk = sub // RADIUS

    def main(lb):
        return pl.BlockSpec((None, tq, WIDTH), lambda i, j: (i, j, lb))

    def prev(lb):
        return pl.BlockSpec((None, RADIUS, WIDTH), lambda i, j: (i, jnp.maximum(j * per - 1, 0), lb))

    def nxt(lb):
        return pl.BlockSpec((None, RADIUS, WIDTH),
                            lambda i, j: (i, jnp.minimum((j + 1) * per, nblk - 1), lb))

    wspec = pl.BlockSpec((1, WIDTH), lambda i, j: (0, 0))
    return pl.pallas_call(
        functools.partial(_attn_kernel, dil=dil, group=group, sub=sub),
        grid=(nseq, nt),
        in_specs=[main(0), main(1), prev(1), nxt(1), main(2), prev(2), nxt(2), wspec, wspec],
        out_specs=[pl.BlockSpec((None, tq, WIDTH), lambda i, j: (i, j, 0)),
                   pl.BlockSpec((None, tq, LANES), lambda i, j: (i, j, 0))],
        out_shape=[jax.ShapeDtypeStruct((nseq, sub, WIDTH), BF16),
                   jax.ShapeDtypeStruct((nseq, sub, LANES), F32)],
        compiler_params=_cparams(("parallel", "parallel")),
        name=f"attn_g{group}",
    )(qkv, qkv, qkv, qkv, qkv, qkv, qkv, qw, kw)


def _merge_kernel(of_ref, ob_ref, z_ref, o0_ref, o1_ref, o2_ref, l0_ref, l1_ref, l2_ref,
                  gate_ref, x_ref, wdn_ref, wat_ref, wo_ref, dnw_ref, n2w_ref, x1_ref, h2_ref):
    tm = x_ref.shape[0]
    oa = of_ref[...].astype(F32) + ob_ref[...].astype(F32)
    z = z_ref[...].astype(F32)
    gated = oa * lax.rsqrt(_head_meansq(oa) + EPS) * dnw_ref[...] * _silu(z)
    ya = _dot(gated.astype(BF16), wdn_ref[...])
    outs = [o0_ref[...].astype(F32)]
    lses = [l0_ref[...]]
    for d, oref, lref in ((4, o1_ref, l1_ref), (16, o2_ref, l2_ref)):
        pinv = _perm_matrix(tm, d, inverse=True)
        outs.append(_dot(pinv, jnp.concatenate([oref[r] for r in range(d)], axis=0)))
        lses.append(_dot3(pinv, jnp.concatenate([lref[r] for r in range(d)], axis=0)))
    m = jnp.maximum(jnp.maximum(lses[0], lses[1]), lses[2])
    es = [jnp.exp(l - m) for l in lses]
    den = es[0] + es[1] + es[2]
    expand = _head_expand()
    ob = jnp.zeros((tm, WIDTH), F32)
    for e, o in zip(es, outs):
        hi, mid, _ = _split3(e / den)
        ob = ob + (_dot(hi, expand) + _dot(mid, expand)) * o
    yb = _dot(ob.astype(BF16), wat_ref[...])
    g = gate_ref[...].astype(F32)
    dm = x_ref.shape[1]
    mixed = _sigmoid(g[:, 0:dm]) * ya + _sigmoid(g[:, dm:2 * dm]) * yb
    x1 = x_ref[...] + _dot(mixed.astype(BF16), wo_ref[...])
    x1_ref[...] = x1
    ms = jnp.mean(x1 * x1, axis=-1, keepdims=True)
    h2_ref[...] = (x1 * lax.rsqrt(ms + EPS) * n2w_ref[...]).astype(BF16)


def _merge(o_dn, pa, attn_o, attn_l, gates, x, wdn, wat, wo, dnw, n2w, tm=256):
    b, s, dm = x.shape
    tok = lambda i, j: (i, j, 0)
    const = lambda i, j: (0, 0)
    in_specs = [
        pl.BlockSpec((None, None, tm, WIDTH), lambda i, j: (0, i, j, 0)),
        pl.BlockSpec((None, None, tm, WIDTH), lambda i, j: (1, i, j, 0)),
        pl.BlockSpec((None, tm, WIDTH), lambda i, j: (i, j, 3)),
        pl.BlockSpec((None, tm, WIDTH), tok),
        pl.BlockSpec((None, 4, tm // 4, WIDTH), lambda i, j: (i, 0, j, 0)),
        pl.BlockSpec((None, 16, tm // 16, WIDTH), lambda i, j: (i, 0, j, 0)),
        pl.BlockSpec((None, tm, LANES), tok),
        pl.BlockSpec((None, 4, tm // 4, LANES), lambda i, j: (i, 0, j, 0)),
        pl.BlockSpec((None, 16, tm // 16, LANES), lambda i, j: (i, 0, j, 0)),
        pl.BlockSpec((None, tm, 2 * dm), tok),
        pl.BlockSpec((None, tm, dm), tok),
        pl.BlockSpec((WIDTH, dm), const),
        pl.BlockSpec((WIDTH, dm), const),
        pl.BlockSpec((dm, dm), const),
        pl.BlockSpec((1, WIDTH), const),
        pl.BlockSpec((1, dm), const),
    ]
    return pl.pallas_call(
        _merge_kernel,
        grid=(b, s // tm),
        in_specs=in_specs,
        out_specs=[pl.BlockSpec((None, tm, dm), tok), pl.BlockSpec((None, tm, dm), tok)],
        out_shape=[jax.ShapeDtypeStruct((b, s, dm), F32), jax.ShapeDtypeStruct((b, s, dm), BF16)],
        compiler_params=_cparams(("parallel", "parallel")),
        name="merge",
    )(o_dn, o_dn, pa, attn_o[0], attn_o[1], attn_o[2], attn_l[0], attn_l[1], attn_l[2],
      gates, x, wdn, wat, wo, dnw, n2w)


def _ffn_kernel(h_ref, hp_ref, hn_ref, wg_ref, wu_ref, cg_ref, cu_ref, wd_ref, x1_ref, o_ref,
                acc_ref, *, tiles_per_seq):
    i = pl.program_id(0)
    j = pl.program_id(1)
    tm = h_ref.shape[0]
    pos = _mod(i, tiles_per_seq)
    has_prev = jnp.where(pos > 0, 1.0, 0.0)
    has_next = jnp.where(pos < tiles_per_seq - 1, 1.0, 0.0)
    lhs = jnp.concatenate([hp_ref[...], h_ref[...], hn_ref[...]], axis=0)

    def conv_branch(w_ref, c_ref):
        ue = _dot(lhs, w_ref[...])
        u = ue[BF16_ROWS:BF16_ROWS + tm]
        prev_row = ue[BF16_ROWS - 1:BF16_ROWS] * has_prev
        next_row = ue[BF16_ROWS + tm:BF16_ROWS + tm + 1] * has_next
        up, dn = _shifted(u, prev_row, next_row)
        cw = c_ref[...]
        return cw[0:1] * up + cw[1:2] * u + cw[2:3] * dn

    act = _silu(conv_branch(wg_ref, cg_ref)) * conv_branch(wu_ref, cu_ref)
    part = _dot(act.astype(BF16), wd_ref[...])

    @pl.when(j == 0)
    def _():
        acc_ref[...] = part

    @pl.when(j > 0)
    def _():
        acc_ref[...] += part

    @pl.when(j == pl.num_programs(1) - 1)
    def _():
        o_ref[...] = x1_ref[...] + acc_ref[...]


def _ffn(h2, x1, w_up, conv_w, w_down, seq, tm=512, tf=1408):
    n, dm = h2.shape
    dff = w_down.shape[0]
    nf = dff // tf
    per = tm // BF16_ROWS
    nblk = n // BF16_ROWS
    return pl.pallas_call(
        functools.partial(_ffn_kernel, tiles_per_seq=seq // tm),
        grid=(n // tm, nf),
        in_specs=[
            pl.BlockSpec((tm, dm), lambda i, j: (i, 0)),
            pl.BlockSpec((BF16_ROWS, dm), lambda i, j: (jnp.maximum(i * per - 1, 0), 0)),
            pl.BlockSpec((BF16_ROWS, dm), lambda i, j: (jnp.minimum((i + 1) * per, nblk - 1), 0)),
            pl.BlockSpec((dm, tf), lambda i, j: (0, j)),
            pl.BlockSpec((dm, tf), lambda i, j: (0, nf + j)),
            pl.BlockSpec((3, tf), lambda i, j: (0, j)),
            pl.BlockSpec((3, tf), lambda i, j: (0, nf + j)),
            pl.BlockSpec((tf, dm), lambda i, j: (j, 0)),
            pl.BlockSpec((tm, dm), lambda i, j: (i, 0)),
        ],
        out_specs=pl.BlockSpec((tm, dm), lambda i, j: (i, 0)),
        out_shape=jax.ShapeDtypeStruct((n, dm), F32),
        scratch_shapes=[pltpu.VMEM((tm, dm), F32)],
        compiler_params=_cparams(("parallel", "arbitrary")),
        name="ffn",
    )(h2, h2, h2, w_up, w_up, conv_w, conv_w, w_down, x1)


def _layer(x, norm1_w, w_in, dn_conv_w, dn_a_log, dn_dt_bias, dn_out_norm_w, attn_q_norm_w,
           attn_k_norm_w, w_dn_out, w_attn_out, w_o, norm2_w, w_ffn_up, ffn_conv_w, w_ffn_down):
    b, s, dm = x.shape
    n = b * s
    w3 = 3 * WIDTH
    c_z = w3
    c_ab = c_z + WIDTH
    c_q = c_ab + 32
    c_k = c_q + w3
    c_v = c_k + w3
    c_gate = c_v + w3
    wb = w_in.astype(BF16)
    w_a = wb[:, 0:c_ab]
    w_ab = jnp.pad(wb[:, c_ab:c_q], ((0, 0), (0, LANES - 32)))
    w_grp = [jnp.concatenate([wb[:, c0 + g * WIDTH:c0 + (g + 1) * WIDTH] for c0 in (c_q, c_k, c_v)], axis=1)
             for g in range(3)]
    w_gate = wb[:, c_gate:c_gate + 2 * dm]

    h, h4, h16, ab = _norm_perm(x, norm1_w.reshape(1, dm), w_ab)
    pa = _matmul(h.reshape(n, dm), w_a, BF16, "proj_dn").reshape(b, s, c_ab)
    gates = _matmul(h.reshape(n, dm), w_gate, BF16, "proj_gate").reshape(b, s, 2 * dm)

    q, k, v, cp = _dn_prep(pa, dn_conv_w, ab, dn_a_log.reshape(1, 16), dn_dt_bias.reshape(1, 16))
    o_dn = _dn_scan(q, k, v, cp, _dn_expand_matrix())

    qw = jnp.tile(attn_q_norm_w, HEADS).reshape(1, WIDTH)
    kw = jnp.tile(attn_k_norm_w, HEADS).reshape(1, WIDTH)
    attn_o, attn_l = [], []
    for g, (d, hg) in enumerate(zip(DILATIONS, (h, h4, h16))):
        qkv = _matmul(hg.reshape(n, dm), w_grp[g], BF16, f"proj_att{g}").reshape(b * d, s // d, w3)
        o, l = _attn(qkv, qw, kw, dil=d, group=g)
        shape = (b, s) if d == 1 else (b, d, s // d)
        attn_o.append(o.reshape(shape + (WIDTH,)))
        attn_l.append(l.reshape(shape + (LANES,)))

    x1, h2 = _merge(o_dn, pa, attn_o, attn_l, gates, x, w_dn_out.astype(BF16), w_attn_out.astype(BF16),
                    w_o.astype(BF16), jnp.tile(dn_out_norm_w, HEADS).reshape(1, WIDTH),
                    norm2_w.reshape(1, dm))
    out = _ffn(h2.reshape(n, dm), x1.reshape(n, dm), w_ffn_up.astype(BF16), ffn_conv_w,
               w_ffn_down.astype(BF16), s)
    return out.reshape(b, s, dm)


def kernel(x, norm1_w, w_in, dn_conv_w, dn_a_log, dn_dt_bias, dn_out_norm_w, attn_q_norm_w, attn_k_norm_w, w_dn_out, w_attn_out, w_o, norm2_w, w_ffn_up, ffn_conv_w, w_ffn_down):
    for layer in range(norm1_w.shape[0]):
        x = _layer(x, norm1_w[layer], w_in[layer], dn_conv_w[layer], dn_a_log[layer], dn_dt_bias[layer],
                   dn_out_norm_w[layer], attn_q_norm_w[layer], attn_k_norm_w[layer], w_dn_out[layer],
                   w_attn_out[layer], w_o[layer], norm2_w[layer], w_ffn_up[layer], ffn_conv_w[layer],
                   w_ffn_down[layer])
    return x
```

```python
import functools

import jax
import jax.numpy as jnp
from jax import lax
from jax.experimental import pallas as pl
from jax.experimental.pallas import tpu as pltpu

F32 = jnp.float32
BF16 = jnp.bfloat16

EPS = 1e-6
NEG_INF = -1e30

HEAD_DIM = 64
HEADS = 8
WIDTH = HEADS * HEAD_DIM
CHUNK = 64
RADIUS = 64
DILATIONS = (1, 4, 16)
ALIBI_MAX = 8.0
N_ATT_HEADS = 24
QUAD = 4 * HEAD_DIM
PHASE_A_PROBLEMS = 8
BF16_ROWS = 16
LANES = 128
V7X_VMEM_LIMIT = 56 * 1024 * 1024


def _cparams(sem):
    return pltpu.CompilerParams(dimension_semantics=sem, vmem_limit_bytes=V7X_VMEM_LIMIT)


def _dot(a, b):
    return jnp.dot(a, b, preferred_element_type=F32)


def _dot_nt(a, b):
    return lax.dot_general(a, b, (((1,), (1,)), ((), ())), preferred_element_type=F32)


def _split3(x):
    hi = x.astype(BF16)
    r1 = x - hi.astype(F32)
    mid = r1.astype(BF16)
    lo = (r1 - mid.astype(F32)).astype(BF16)
    return hi, mid, lo


def _dot3(a, x):
    hi, mid, lo = _split3(x)
    return _dot(a, hi) + _dot(a, mid) + _dot(a, lo)


def _log2(n):
    assert n > 0 and n & (n - 1) == 0, n
    return n.bit_length() - 1


def _div(x, n):
    return lax.shift_right_logical(x, _log2(n))


def _mod(x, n):
    assert n & (n - 1) == 0, n
    return x & (n - 1)


def _iota(shape, dim):
    return lax.broadcasted_iota(jnp.int32, shape, dim)


def _sigmoid(x):
    return 1.0 / (1.0 + jnp.exp(-x))


def _silu(x):
    return x * _sigmoid(x)


def _head_ones(width):
    r = _div(_iota((width, width), 0), HEAD_DIM)
    c = _div(_iota((width, width), 1), HEAD_DIM)
    return (r == c).astype(BF16)


def _head_meansq(x):
    return _dot((x * x).astype(BF16), _head_ones(x.shape[1])) * (1.0 / HEAD_DIM)


def _head_expand():
    r = _iota((LANES, WIDTH), 0)
    c = _div(_iota((LANES, WIDTH), 1), HEAD_DIM)
    return (r == c).astype(BF16)


def _perm_matrix(tm, d, inverse=False):
    run = tm // d
    row = _iota((tm, tm), 0)
    col = _iota((tm, tm), 1)
    if inverse:
        src = _mod(row, d) * run + _div(row, d)
    else:
        src = _mod(row, run) * d + _div(row, run)
    return (col == src).astype(BF16)


def _norm_perm_kernel(x_ref, nw_ref, wab_ref, h_ref, h4_ref, h16_ref, ab_ref):
    x = x_ref[...]
    ms = jnp.mean(x * x, axis=-1, keepdims=True)
    h = (x * lax.rsqrt(ms + EPS) * nw_ref[...]).astype(BF16)
    h_ref[...] = h
    tm = x.shape[0]
    for d, ref in ((4, h4_ref), (16, h16_ref)):
        run = tm // d
        hp = _dot(_perm_matrix(tm, d), h).astype(BF16)
        for r in range(d):
            ref[r] = hp[r * run:(r + 1) * run]
    ab_ref[...] = _dot(h, wab_ref[...])


def _norm_perm(x, norm_w, wab, tm=256):
    b, s, dm = x.shape
    return pl.pallas_call(
        _norm_perm_kernel,
        grid=(b, s // tm),
        in_specs=[
            pl.BlockSpec((None, tm, dm), lambda i, j: (i, j, 0)),
            pl.BlockSpec((1, dm), lambda i, j: (0, 0)),
            pl.BlockSpec((dm, LANES), lambda i, j: (0, 0)),
        ],
        out_specs=[
            pl.BlockSpec((None, tm, dm), lambda i, j: (i, j, 0)),
            pl.BlockSpec((None, 4, tm // 4, dm), lambda i, j: (i, 0, j, 0)),
            pl.BlockSpec((None, 16, tm // 16, dm), lambda i, j: (i, 0, j, 0)),
            pl.BlockSpec((None, tm, LANES), lambda i, j: (i, j, 0)),
        ],
        out_shape=[
            jax.ShapeDtypeStruct((b, s, dm), BF16),
            jax.ShapeDtypeStruct((b, 4, s // 4, dm), BF16),
            jax.ShapeDtypeStruct((b, 16, s // 16, dm), BF16),
            jax.ShapeDtypeStruct((b, s, LANES), F32),
        ],
        compiler_params=_cparams(("parallel", "parallel")),
        name="norm_perm",
    )(x, norm_w, wab)


def _mm_kernel(a_ref, w_ref, o_ref):
    o_ref[...] = _dot(a_ref[...], w_ref[...]).astype(o_ref.dtype)


def _matmul(a, w, out_dtype, name, tm=1024, tn=1024):
    m, k = a.shape
    n = w.shape[1]
    tm = min(tm, m)
    tn = min(tn, n)
    while n % tn:
        tn //= 2
    return pl.pallas_call(
        _mm_kernel,
        grid=(m // tm, n // tn),
        in_specs=[pl.BlockSpec((tm, k), lambda i, j: (i, 0)),
                  pl.BlockSpec((k, tn), lambda i, j: (0, j))],
        out_specs=pl.BlockSpec((tm, tn), lambda i, j: (i, j)),
        out_shape=jax.ShapeDtypeStruct((m, n), out_dtype),
        compiler_params=_cparams(("parallel", "parallel")),
        name=name,
    )(a, w)


def _shifted(x, prev_row, next_row):
    t = x.shape[0]
    row = _iota((t, 1), 0)
    up = jnp.where(row == 0, prev_row, pltpu.roll(x, 1, 0))
    dn = jnp.where(row == t - 1, next_row, pltpu.roll(x, t - 1, 0))
    return up, dn


def _dn_prep_kernel(x_ref, xp_ref, xn_ref, cw_ref, ab_ref, alog_ref, dtb_ref,
                    q_ref, k_ref, v_ref, cp_ref):
    j = pl.program_id(1)
    nj = pl.num_programs(1)
    x = x_ref[...].astype(F32)
    t = x.shape[0]
    prev_row = xp_ref[BF16_ROWS - 1:BF16_ROWS, :].astype(F32) * jnp.where(j > 0, 1.0, 0.0)
    next_row = xn_ref[0:1, :].astype(F32) * jnp.where(j < nj - 1, 1.0, 0.0)
    up, dn = _shifted(x, prev_row, next_row)
    cw = cw_ref[...]
    y = _silu(cw[0:1] * up + cw[1:2] * x + cw[2:3] * dn)
    q = y[:, 0:WIDTH]
    k = y[:, WIDTH:2 * WIDTH]
    q_ref[...] = (q * lax.rsqrt(_head_meansq(q) * HEAD_DIM + EPS)).astype(BF16)
    k_ref[...] = (k * lax.rsqrt(_head_meansq(k) * HEAD_DIM + EPS)).astype(BF16)
    v_ref[...] = y[:, 2 * WIDTH:3 * WIDTH].astype(BF16)

    ab = ab_ref[...]
    a = ab[:, 0:16] + dtb_ref[...]
    softplus = jnp.maximum(a, 0.0) + jnp.log(1.0 + jnp.exp(-jnp.abs(a)))
    g = -jnp.exp(alog_ref[...]) * softplus
    beta = _sigmoid(ab[:, 16:32])
    row = _iota((t, t), 0)
    col = _iota((t, t), 1)
    same = _div(row, CHUNK) == _div(col, CHUNK)
    lower = (same & (row >= col)).astype(BF16)
    upper = (same & (row <= col)).astype(BF16)
    lane = _iota((t, 16), 1)
    gc = jnp.where(lane < HEADS, _dot3(lower, g), _dot3(upper, g))
    prow = _iota((16, LANES), 0)
    pcol = _iota((16, LANES), 1)
    acc = jnp.zeros((t, LANES), F32)
    for idx, piece in enumerate(_split3(gc) + _split3(beta)):
        place = (pcol == prow + 16 * idx).astype(BF16)
        acc = acc + _dot(piece, place)
    cp_ref[...] = acc.astype(BF16)


def _dn_prep(pa, conv_w, ab, alog, dtb, tp=256):
    b, s, _ = pa.shape
    w3 = 3 * WIDTH
    nblk = s // BF16_ROWS
    per = tp // BF16_ROWS
    return pl.pallas_call(
        _dn_prep_kernel,
        grid=(b, s // tp),
        in_specs=[
            pl.BlockSpec((None, tp, w3), lambda i, j: (i, j, 0)),
            pl.BlockSpec((None, BF16_ROWS, w3), lambda i, j: (i, jnp.maximum(j * per - 1, 0), 0)),
            pl.BlockSpec((None, BF16_ROWS, w3), lambda i, j: (i, jnp.minimum((j + 1) * per, nblk - 1), 0)),
            pl.BlockSpec((3, w3), lambda i, j: (0, 0)),
            pl.BlockSpec((None, tp, LANES), lambda i, j: (i, j, 0)),
            pl.BlockSpec((1, 16), lambda i, j: (0, 0)),
            pl.BlockSpec((1, 16), lambda i, j: (0, 0)),
        ],
        out_specs=[pl.BlockSpec((None, tp, WIDTH), lambda i, j: (i, j, 0))] * 3
        + [pl.BlockSpec((None, tp, LANES), lambda i, j: (i, j, 0))],
        out_shape=[jax.ShapeDtypeStruct((b, s, WIDTH), BF16)] * 3
        + [jax.ShapeDtypeStruct((b, s, LANES), BF16)],
        compiler_params=_cparams(("parallel", "parallel")),
        name="dn_prep",
    )(pa, pa, pa, conv_w, ab, alog, dtb)


def _block_diag(x):
    blk = _div(_iota(x.shape, 1), HEAD_DIM)
    parts = [jnp.where(blk == a, x, 0.0) for a in range(4)]
    return jnp.concatenate(parts, axis=0).astype(BF16)


def _dn_scan_kernel(q_ref, k_ref, v_ref, cp_ref, e2_ref, o_ref,
                    s_scr, u_scr, l1_scr, l2_scr, al_scr):
    dirn = pl.program_id(1)
    seg = pl.program_id(2)
    nb = q_ref.shape[0]
    nchunk = q_ref.shape[1] // CHUNK
    fwd = dirn == 0

    @pl.when(seg == 0)
    def _():
        s_scr[...] = jnp.zeros_like(s_scr)

    row = _iota((CHUNK, QUAD), 0)
    col = _mod(_iota((CHUNK, QUAD), 1), HEAD_DIM)
    later = jnp.where(fwd, row, col)
    earlier = jnp.where(fwd, col, row)
    incl = later >= earlier
    strict = later > earlier
    diag = row == col
    eye = diag.astype(BF16)
    unroll = max(1, PHASE_A_PROBLEMS // (2 * nb))

    def phase_a(it, carry):
        probs = []
        for u in range(unroll):
            c = it * unroll + u
            rows = pl.ds(pl.multiple_of(c * CHUNK, CHUNK), CHUNK)
            for bi in range(nb):
                x = _dot(cp_ref[bi, rows, :], e2_ref[...])
                for qd in range(2):
                    lanes = slice(qd * QUAD, (qd + 1) * QUAD)
                    probs.append(dict(
                        c=c, slot=2 * bi + qd, gcol=x[:, qd * QUAD:(qd + 1) * QUAD],
                        beta=x[:, WIDTH + qd * QUAD:WIDTH + (qd + 1) * QUAD],
                        kq=k_ref[bi, rows, lanes], qq=q_ref[bi, rows, lanes], vq=v_ref[bi, rows, lanes]))
        for pr in probs:
            pr["kf"] = pr["kq"].astype(F32)
            pr["z"] = _dot_nt(jnp.concatenate([pr["kq"], pr["qq"], eye], axis=0),
                              _block_diag(pr["kf"]))
        for pr in probs:
            gcol, beta, z = pr["gcol"], pr["beta"], pr["z"]
            grow = jnp.sum(jnp.where(diag, gcol, 0.0), axis=0, keepdims=True)
            glast = jnp.where(fwd, gcol[CHUNK - 1:CHUNK, :], gcol[0:1, :])
            dm = jnp.exp(jnp.where(incl, gcol - grow, NEG_INF))
            eg = jnp.exp(gcol)
            a = jnp.where(strict, beta * z[0:CHUNK] * dm, 0.0)
            qkm = jnp.where(incl, z[CHUNK:2 * CHUNK] * dm, 0.0) * 0.125
            kdt = z[2 * CHUNK:3 * CHUNK] * jnp.exp(glast - grow)
            pr["vb"] = pr["vq"].astype(F32) * beta
            pr["kbg"] = pr["kf"] * (beta * eg)
            qdec = pr["qq"].astype(F32) * (eg * 0.125)
            c, slot = pr["c"], pr["slot"]
            l2_scr[slot, c] = jnp.concatenate([qkm, kdt], axis=0).astype(BF16)
            l1_scr[slot, c, CHUNK:2 * CHUNK, :] = qdec.astype(BF16)
            al_scr[slot, c] = jnp.broadcast_to(jnp.exp(glast), (8, QUAD))
            pr["a"] = a
        for pr in probs:
            pr["p"] = _dot(pr["a"].astype(BF16), _block_diag(pr["a"]))
            pr["n"] = -pr["a"]
        for _ in range(4):
            for pr in probs:
                n, p = pr["n"], pr["p"]
                zz = _dot(jnp.concatenate([n, p], axis=0).astype(BF16), _block_diag(p))
                pr["n"] = n + p + zz[0:CHUNK]
                pr["p"] = zz[CHUNK:2 * CHUNK]
        for pr in probs:
            n, p = pr["n"], pr["p"]
            pr["n"] = n + p + _dot(n.astype(BF16), _block_diag(p))
        for pr in probs:
            vb, kbg, c, slot = pr["vb"], pr["kbg"], pr["c"], pr["slot"]
            rhs = jnp.concatenate([_block_diag(vb), _block_diag(kbg)], axis=1)
            uw = _dot(pr["n"].astype(BF16), rhs)
            u_scr[slot, c] = vb + uw[:, 0:QUAD]
            l1_scr[slot, c, 0:CHUNK, :] = (kbg + uw[:, QUAD:2 * QUAD]).astype(BF16)
        return carry

    lax.fori_loop(0, nchunk // unroll, phase_a, 0)

    nslot = 2 * nb

    def phase_b(i, states):
        c = jnp.where(fwd, i, nchunk - 1 - i)
        r0 = pl.multiple_of(c * CHUNK, CHUNK)
        z1 = [_dot(l1_scr[sl, c], _block_diag(states[sl])) for sl in range(nslot)]
        vn = [u_scr[sl, c] - z1[sl][0:CHUNK] for sl in range(nslot)]
        z2 = [_dot(l2_scr[sl, c], _block_diag(vn[sl])) for sl in range(nslot)]
        new = []
        for sl in range(nslot):
            o = z1[sl][CHUNK:2 * CHUNK] + z2[sl][0:CHUNK]
            o_ref[sl // 2, pl.ds(r0, CHUNK), (sl % 2) * QUAD:(sl % 2 + 1) * QUAD] = o.astype(o_ref.dtype)
            new.append(al_scr[sl, c][0:1, :] * states[sl] + z2[sl][CHUNK:2 * CHUNK])
        return tuple(new)

    final = lax.fori_loop(0, nchunk, phase_b, tuple(s_scr[sl] for sl in range(nslot)))
    for sl in range(nslot):
        s_scr[sl] = final[sl]


def _dn_scan(q, k, v, cp, e2, nb=4, seg=512):
    b, s, _ = q.shape
    while b % nb:
        nb //= 2
    seg = min(seg, s)
    nseg = s // seg
    nchunk = seg // CHUNK

    def tok(i, d, j):
        return (i, jnp.where(d == 0, j, nseg - 1 - j), 0)

    return pl.pallas_call(
        _dn_scan_kernel,
        grid=(b // nb, 2, nseg),
        in_specs=[pl.BlockSpec((nb, seg, WIDTH), tok)] * 3
        + [pl.BlockSpec((nb, seg, LANES), tok),
           pl.BlockSpec((None, LANES, 2 * WIDTH), lambda i, d, j: (d, 0, 0))],
        out_specs=pl.BlockSpec((None, nb, seg, WIDTH),
                               lambda i, d, j: (d, i, jnp.where(d == 0, j, nseg - 1 - j), 0)),
        out_shape=jax.ShapeDtypeStruct((2, b, s, WIDTH), BF16),
        scratch_shapes=[
            pltpu.VMEM((2 * nb, CHUNK, QUAD), F32),
            pltpu.VMEM((2 * nb, nchunk, CHUNK, QUAD), F32),
            pltpu.VMEM((2 * nb, nchunk, 2 * CHUNK, QUAD), BF16),
            pltpu.VMEM((2 * nb, nchunk, 2 * CHUNK, QUAD), BF16),
            pltpu.VMEM((2 * nb, nchunk, 8, QUAD), F32),
        ],
        compiler_params=_cparams(("parallel", "arbitrary", "arbitrary")),
        name="dn_scan",
    )(q, k, v, cp, e2)


def _dn_expand_matrix():
    r = jnp.arange(LANES)[:, None]
    c = jnp.arange(2 * WIDTH)[None, :]
    piece, lane = r // 16, r % 16
    out = []
    for d in range(2):
        head = lane - 8 * d
        ok = (piece < 6) & (head >= 0) & (head < HEADS)
        ok = ok & ((c // WIDTH) == (piece // 3)) & (((c % WIDTH) // HEAD_DIM) == head)
        out.append(ok)
    return jnp.stack(out).astype(BF16)


QBLK = 2 * RADIUS


def _attn_kernel(q_ref, k_ref, kp_ref, kn_ref, v_ref, vp_ref, vn_ref, qw_ref, kw_ref,
                 o_ref, lse_ref, *, dil, group, sub):
    t = pl.program_id(1)
    tq = q_ref.shape[0]
    q = q_ref[...].astype(F32)
    qn = (q * lax.rsqrt(_head_meansq(q) + EPS) * (qw_ref[...] * HEAD_DIM ** -0.5)).astype(BF16)
    kx = jnp.concatenate([kp_ref[...], k_ref[...], kn_ref[...]], axis=0).astype(F32)
    kn = (kx * lax.rsqrt(_head_meansq(kx) + EPS) * kw_ref[...]).astype(BF16)
    vx = jnp.concatenate([vp_ref[...], v_ref[...], vn_ref[...]], axis=0)
    a = _iota((QBLK, 2 * QBLK), 0)
    c = _iota((QBLK, 2 * QBLK), 1)
    rel = jnp.abs(a + RADIUS - c)
    base = (rel * (-dil)).astype(F32)
    lane = _iota((1, LANES), 1)
    for qb in range(tq // QBLK):
        kj = t * tq + (qb * QBLK - RADIUS) + c
        valid = (rel <= RADIUS) & (kj >= 0) & (kj < sub)
        rows = slice(qb * QBLK, (qb + 1) * QBLK)
        krows = slice(qb * QBLK, qb * QBLK + 2 * QBLK)
        lse_tile = jnp.zeros((QBLK, LANES), F32)
        heads = [slice(h * HEAD_DIM, (h + 1) * HEAD_DIM) for h in range(HEADS)]
        scores = [_dot_nt(qn[rows, hl], kn[krows, hl]) for hl in heads]
        probs, denoms = [], []
        for h in range(HEADS):
            slope = 2.0 ** (-ALIBI_MAX * (group * HEADS + h + 1) / N_ATT_HEADS)
            sc = jnp.where(valid, scores[h] + slope * base, NEG_INF)
            m = jnp.max(sc, axis=-1, keepdims=True)
            p = jnp.exp(sc - m)
            l = jnp.sum(p, axis=-1, keepdims=True)
            probs.append(p.astype(BF16))
            denoms.append(l)
            lse_tile = lse_tile + jnp.where(lane == h, m + jnp.log(l), 0.0)
        outs = [_dot(probs[h], vx[krows, heads[h]]) for h in range(HEADS)]
        for h in range(HEADS):
            o_ref[rows, heads[h]] = (outs[h] / denoms[h]).astype(o_ref.dtype)
        lse_ref[rows, :] = lse_tile


def _attn(qkv, qw, kw, *, dil, group):
    nseq, sub, _ = qkv.shape
    tq = min(256, sub)
    nt = sub // tq
    per = tq // RADIUS
    nblk = sub // RADIUS

    def main(lb):
        return pl.BlockSpec((None, tq, WIDTH), lambda i, j: (i, j, lb))

    def prev(lb):
        return pl.BlockSpec((None, RADIUS, WIDTH), lambda i, j: (i, jnp.maximum(j * per - 1, 0), lb))

    def nxt(lb):
        return pl.BlockSpec((None, RADIUS, WIDTH),
                            lambda i, j: (i, jnp.minimum((j + 1) * per, nblk - 1), lb))

    wspec = pl.BlockSpec((1, WIDTH), lambda i, j: (0, 0))
    return pl.pallas_call(
        functools.partial(_attn_kernel, dil=dil, group=group, sub=sub),
        grid=(nseq, nt),
        in_specs=[main(0), main(1), prev(1), nxt(1), main(2), prev(2), nxt(2), wspec, wspec],
        out_specs=[pl.BlockSpec((None, tq, WIDTH), lambda i, j: (i, j, 0)),
                   pl.BlockSpec((None, tq, LANES), lambda i, j: (i, j, 0))],
        out_shape=[jax.ShapeDtypeStruct((nseq, sub, WIDTH), BF16),
                   jax.ShapeDtypeStruct((nseq, sub, LANES), F32)],
        compiler_params=_cparams(("parallel", "parallel")),
        name=f"attn_g{group}",
    )(qkv, qkv, qkv, qkv, qkv, qkv, qkv, qw, kw)


def _merge_kernel(of_ref, ob_ref, z_ref, o0_ref, o1_ref, o2_ref, l0_ref, l1_ref, l2_ref,
                  gate_ref, x_ref, wdn_ref, wat_ref, wo_ref, dnw_ref, n2w_ref, x1_ref, h2_ref):
    tm = x_ref.shape[0]
    oa = of_ref[...].astype(F32) + ob_ref[...].astype(F32)
    z = z_ref[...].astype(F32)
    gated = oa * lax.rsqrt(_head_meansq(oa) + EPS) * dnw_ref[...] * _silu(z)
    ya = _dot(gated.astype(BF16), wdn_ref[...])
    outs = [o0_ref[...].astype(F32)]
    lses = [l0_ref[...]]
    for d, oref, lref in ((4, o1_ref, l1_ref), (16, o2_ref, l2_ref)):
        pinv = _perm_matrix(tm, d, inverse=True)
        outs.append(_dot(pinv, jnp.concatenate([oref[r] for r in range(d)], axis=0)))
        lses.append(_dot3(pinv, jnp.concatenate([lref[r] for r in range(d)], axis=0)))
    m = jnp.maximum(jnp.maximum(lses[0], lses[1]), lses[2])
    es = [jnp.exp(l - m) for l in lses]
    den = es[0] + es[1] + es[2]
    expand = _head_expand()
    ob = jnp.zeros((tm, WIDTH), F32)
    for e, o in zip(es, outs):
        hi, mid, _ = _split3(e / den)
        ob = ob + (_dot(hi, expand) + _dot(mid, expand)) * o
    yb = _dot(ob.astype(BF16), wat_ref[...])
    g = gate_ref[...].astype(F32)
    dm = x_ref.shape[1]
    mixed = _sigmoid(g[:, 0:dm]) * ya + _sigmoid(g[:, dm:2 * dm]) * yb
    x1 = x_ref[...] + _dot(mixed.astype(BF16), wo_ref[...])
    x1_ref[...] = x1
    ms = jnp.mean(x1 * x1, axis=-1, keepdims=True)
    h2_ref[...] = (x1 * lax.rsqrt(ms + EPS) * n2w_ref[...]).astype(BF16)


def _merge(o_dn, pa, attn_o, attn_l, gates, x, wdn, wat, wo, dnw, n2w, tm=256):
    b, s, dm = x.shape
    tok = lambda i, j: (i, j, 0)
    const = lambda i, j: (0, 0)
    in_specs = [
        pl.BlockSpec((None, None, tm, WIDTH), lambda i, j: (0, i, j, 0)),
        pl.BlockSpec((None, None, tm, WIDTH), lambda i, j: (1, i, j, 0)),
        pl.BlockSpec((None, tm, WIDTH), lambda i, j: (i, j, 3)),
        pl.BlockSpec((None, tm, WIDTH), tok),
        pl.BlockSpec((None, 4, tm // 4, WIDTH), lambda i, j: (i, 0, j, 0)),
        pl.BlockSpec((None, 16, tm // 16, WIDTH), lambda i, j: (i, 0, j, 0)),
        pl.BlockSpec((None, tm, LANES), tok),
        pl.BlockSpec((None, 4, tm // 4, LANES), lambda i, j: (i, 0, j, 0)),
        pl.BlockSpec((None, 16, tm // 16, LANES), lambda i, j: (i, 0, j, 0)),
        pl.BlockSpec((None, tm, 2 * dm), tok),
        pl.BlockSpec((None, tm, dm), tok),
        pl.BlockSpec((WIDTH, dm), const),
        pl.BlockSpec((WIDTH, dm), const),
        pl.BlockSpec((dm, dm), const),
        pl.BlockSpec((1, WIDTH), const),
        pl.BlockSpec((1, dm), const),
    ]
    return pl.pallas_call(
        _merge_kernel,
        grid=(b, s // tm),
        in_specs=in_specs,
        out_specs=[pl.BlockSpec((None, tm, dm), tok), pl.BlockSpec((None, tm, dm), tok)],
        out_shape=[jax.ShapeDtypeStruct((b, s, dm), F32), jax.ShapeDtypeStruct((b, s, dm), BF16)],
        compiler_params=_cparams(("parallel", "parallel")),
        name="merge",
    )(o_dn, o_dn, pa, attn_o[0], attn_o[1], attn_o[2], attn_l[0], attn_l[1], attn_l[2],
      gates, x, wdn, wat, wo, dnw, n2w)


def _ffn_kernel(h_ref, hp_ref, hn_ref, wg_ref, wu_ref, cg_ref, cu_ref, wd_ref, x1_ref, o_ref,
                acc_ref, *, tiles_per_seq):
    i = pl.program_id(0)
    j = pl.program_id(1)
    tm = h_ref.shape[0]
    pos = _mod(i, tiles_per_seq)
    has_prev = jnp.where(pos > 0, 1.0, 0.0)
    has_next = jnp.where(pos < tiles_per_seq - 1, 1.0, 0.0)
    lhs = jnp.concatenate([hp_ref[...], h_ref[...], hn_ref[...]], axis=0)

    def conv_branch(w_ref, c_ref):
        ue = _dot(lhs, w_ref[...])
        u = ue[BF16_ROWS:BF16_ROWS + tm]
        prev_row = ue[BF16_ROWS - 1:BF16_ROWS] * has_prev
        next_row = ue[BF16_ROWS + tm:BF16_ROWS + tm + 1] * has_next
        up, dn = _shifted(u, prev_row, next_row)
        cw = c_ref[...]
        return cw[0:1] * up + cw[1:2] * u + cw[2:3] * dn

    act = _silu(conv_branch(wg_ref, cg_ref)) * conv_branch(wu_ref, cu_ref)
    part = _dot(act.astype(BF16), wd_ref[...])

    @pl.when(j == 0)
    def _():
        acc_ref[...] = part

    @pl.when(j > 0)
    def _():
        acc_ref[...] += part

    @pl.when(j == pl.num_programs(1) - 1)
    def _():
        o_ref[...] = x1_ref[...] + acc_ref[...]


def _ffn(h2, x1, w_up, conv_w, w_down, seq, tm=512, tf=1408):
    n, dm = h2.shape
    dff = w_down.shape[0]
    nf = dff // tf
    per = tm // BF16_ROWS
    nblk = n // BF16_ROWS
    return pl.pallas_call(
        functools.partial(_ffn_kernel, tiles_per_seq=seq // tm),
        grid=(n // tm, nf),
        in_specs=[
            pl.BlockSpec((tm, dm), lambda i, j: (i, 0)),
            pl.BlockSpec((BF16_ROWS, dm), lambda i, j: (jnp.maximum(i * per - 1, 0), 0)),
            pl.BlockSpec((BF16_ROWS, dm), lambda i, j: (jnp.minimum((i + 1) * per, nblk - 1), 0)),
            pl.BlockSpec((dm, tf), lambda i, j: (0, j)),
            pl.BlockSpec((dm, tf), lambda i, j: (0, nf + j)),
            pl.BlockSpec((3, tf), lambda i, j: (0, j)),
            pl.BlockSpec((3, tf), lambda i, j: (0, nf + j)),
            pl.BlockSpec((tf, dm), lambda i, j: (j, 0)),
            pl.BlockSpec((tm, dm), lambda i, j: (i, 0)),
        ],
        out_specs=pl.BlockSpec((tm, dm), lambda i, j: (i, 0)),
        out_shape=jax.ShapeDtypeStruct((n, dm), F32),
        scratch_shapes=[pltpu.VMEM((tm, dm), F32)],
        compiler_params=_cparams(("parallel", "arbitrary")),
        name="ffn",
    )(h2, h2, h2, w_up, w_up, conv_w, conv_w, w_down, x1)


def _layer(x, norm1_w, w_in, dn_conv_w, dn_a_log, dn_dt_bias, dn_out_norm_w, attn_q_norm_w,
           attn_k_norm_w, w_dn_out, w_attn_out, w_o, norm2_w, w_ffn_up, ffn_conv_w, w_ffn_down):
    b, s, dm = x.shape
    n = b * s
    w3 = 3 * WIDTH
    c_z = w3
    c_ab = c_z + WIDTH
    c_q = c_ab + 32
    c_k = c_q + w3
    c_v = c_k + w3
    c_gate = c_v + w3
    wb = w_in.astype(BF16)
    w_a = wb[:, 0:c_ab]
    w_ab = jnp.pad(wb[:, c_ab:c_q], ((0, 0), (0, LANES - 32)))
    w_grp = [jnp.concatenate([wb[:, c0 + g * WIDTH:c0 + (g + 1) * WIDTH] for c0 in (c_q, c_k, c_v)], axis=1)
             for g in range(3)]
    w_gate = wb[:, c_gate:c_gate + 2 * dm]

    h, h4, h16, ab = _norm_perm(x, norm1_w.reshape(1, dm), w_ab)
    pa = _matmul(h.reshape(n, dm), w_a, BF16, "proj_dn").reshape(b, s, c_ab)
    gates = _matmul(h.reshape(n, dm), w_gate, BF16, "proj_gate").reshape(b, s, 2 * dm)

    q, k, v, cp = _dn_prep(pa, dn_conv_w, ab, dn_a_log.reshape(1, 16), dn_dt_bias.reshape(1, 16))
    o_dn = _dn_scan(q, k, v, cp, _dn_expand_matrix())

    qw = jnp.tile(attn_q_norm_w, HEADS).reshape(1, WIDTH)
    kw = jnp.tile(attn_k_norm_w, HEADS).reshape(1, WIDTH)
    attn_o, attn_l = [], []
    for g, (d, hg) in enumerate(zip(DILATIONS, (h, h4, h16))):
        qkv = _matmul(hg.reshape(n, dm), w_grp[g], BF16, f"proj_att{g}").reshape(b * d, s // d, w3)
        o, l = _attn(qkv, qw, kw, dil=d, group=g)
        shape = (b, s) if d == 1 else (b, d, s // d)
        attn_o.append(o.reshape(shape + (WIDTH,)))
        attn_l.append(l.reshape(shape + (LANES,)))

    x1, h2 = _merge(o_dn, pa, attn_o, attn_l, gates, x, w_dn_out.astype(BF16), w_attn_out.astype(BF16),
                    w_o.astype(BF16), jnp.tile(dn_out_norm_w, HEADS).reshape(1, WIDTH),
                    norm2_w.reshape(1, dm))
    out = _ffn(h2.reshape(n, dm), x1.reshape(n, dm), w_ffn_up.astype(BF16), ffn_conv_w,
               w_ffn_down.astype(BF16), s)
    return out.reshape(b, s, dm)


def kernel(x, norm1_w, w_in, dn_conv_w, dn_a_log, dn_dt_bias, dn_out_norm_w, attn_q_norm_w, attn_k_norm_w, w_dn_out, w_attn_out, w_o, norm2_w, w_ffn_up, ffn_conv_w, w_ffn_down):
    for layer in range(norm1_w.shape[0]):
        x = _layer(x, norm1_w[layer], w_in[layer], dn_conv_w[layer], dn_a_log[layer], dn_dt_bias[layer],
                   dn_out_norm_w[layer], attn_q_norm_w[layer], attn_k_norm_w[layer], w_dn_out[layer],
                   w_attn_out[layer], w_o[layer], norm2_w[layer], w_ffn_up[layer], ffn_conv_w[layer],
                   w_ffn_down[layer])
    return x
```

```python
import functools

import jax
import jax.numpy as jnp
from jax import lax
from jax.experimental import pallas as pl
from jax.experimental.pallas import tpu as pltpu

F32 = jnp.float32
BF16 = jnp.bfloat16

EPS = 1e-6
NEG_INF = -1e30
LOG2E = 1.4426950408889634
LN2 = 0.6931471805599453

HEAD_DIM = 64
HEADS = 8
WIDTH = HEADS * HEAD_DIM
CHUNK = 64
RADIUS = 64
DILATIONS = (1, 4, 16)
ALIBI_MAX = 8.0
N_ATT_HEADS = 24
QUAD = 4 * HEAD_DIM
PHASE_A_PROBLEMS = 8
BF16_ROWS = 16
LANES = 128
V7X_VMEM_LIMIT = 56 * 1024 * 1024


def _cparams(sem):
    return pltpu.CompilerParams(dimension_semantics=sem, vmem_limit_bytes=V7X_VMEM_LIMIT)


def _dot(a, b):
    return jnp.dot(a, b, preferred_element_type=F32)


def _dot_nt(a, b):
    return lax.dot_general(a, b, (((1,), (1,)), ((), ())), preferred_element_type=F32)


def _split3(x):
    hi = x.astype(BF16)
    r1 = x - hi.astype(F32)
    mid = r1.astype(BF16)
    lo = (r1 - mid.astype(F32)).astype(BF16)
    return hi, mid, lo


def _dot3(a, x):
    hi, mid, lo = _split3(x)
    return _dot(a, hi) + _dot(a, mid) + _dot(a, lo)


def _log2(n):
    assert n > 0 and n & (n - 1) == 0, n
    return n.bit_length() - 1


def _div(x, n):
    return lax.shift_right_logical(x, _log2(n))


def _mod(x, n):
    assert n & (n - 1) == 0, n
    return x & (n - 1)


def _iota(shape, dim):
    return lax.broadcasted_iota(jnp.int32, shape, dim)


def _sigmoid(x):
    return 1.0 / (1.0 + jnp.exp(-x))


def _silu(x):
    return x * _sigmoid(x)


def _head_mean_matrix():
    r = jnp.arange(WIDTH)[:, None] // HEAD_DIM
    c = jnp.arange(WIDTH)[None, :] // HEAD_DIM
    return jnp.where(r == c, 1.0 / HEAD_DIM, 0.0).astype(BF16)


def _head_meansq(x, hm):
    return _dot((x * x).astype(BF16), hm)


def _lse_lane(h):
    return h if h % 2 == 0 else HEAD_DIM + h


def _head_expand():
    r = _iota((LANES, WIDTH), 0)
    head = _div(_iota((LANES, WIDTH), 1), HEAD_DIM)
    lane = jnp.where(_mod(head, 2) == 0, head, head + HEAD_DIM)
    return (r == lane).astype(BF16)


def _perm_matrix(tm, d, inverse=False):
    run = tm // d
    row = _iota((tm, tm), 0)
    col = _iota((tm, tm), 1)
    if inverse:
        src = _mod(row, d) * run + _div(row, d)
    else:
        src = _mod(row, run) * d + _div(row, run)
    return (col == src).astype(BF16)


def _in_proj_kernel(x_ref, nw_ref, wa_ref, wgate_ref, wab_ref, wg0_ref, wg1_ref, wg2_ref,
                    pa_ref, gate_ref, ab_ref, q0_ref, q1_ref, q2_ref):
    x = x_ref[...]
    ms = jnp.mean(x * x, axis=-1, keepdims=True)
    h = (x * lax.rsqrt(ms + EPS) * nw_ref[...]).astype(BF16)
    tm = x.shape[0]
    h4 = _dot(_perm_matrix(tm, 4), h).astype(BF16)
    h16 = _dot(_perm_matrix(tm, 16), h).astype(BF16)
    pa_ref[...] = _dot(h, wa_ref[...]).astype(BF16)
    gate_ref[...] = _dot(h, wgate_ref[...]).astype(BF16)
    q0_ref[...] = _dot(h, wg0_ref[...]).astype(BF16)
    ab_ref[...] = _dot(h, wab_ref[...])
    for d, hp, w_ref, ref in ((4, h4, wg1_ref, q1_ref), (16, h16, wg2_ref, q2_ref)):
        run = tm // d
        y = _dot(hp, w_ref[...]).astype(BF16)
        for r in range(d):
            ref[r] = y[r * run:(r + 1) * run]


def _in_proj(x, norm_w, w_a, w_gate, w_ab, w_grp, tm=256):
    b, s, dm = x.shape
    w3 = 3 * WIDTH
    tok = lambda i, j: (i, j, 0)

    def resident(shape):
        return pl.BlockSpec(shape, lambda i, j: (0, 0), pipeline_mode=pl.Buffered(1))

    return pl.pallas_call(
        _in_proj_kernel,
        grid=(b, s // tm),
        in_specs=[pl.BlockSpec((None, tm, dm), tok), resident((1, dm)), resident(w_a.shape),
                  resident(w_gate.shape), resident(w_ab.shape)] + [resident((dm, w3))] * 3,
        out_specs=[
            pl.BlockSpec((None, tm, w_a.shape[1]), tok),
            pl.BlockSpec((None, tm, 2 * dm), tok),
            pl.BlockSpec((None, tm, LANES), tok),
            pl.BlockSpec((None, tm, w3), tok),
            pl.BlockSpec((None, 4, tm // 4, w3), lambda i, j: (i, 0, j, 0)),
            pl.BlockSpec((None, 16, tm // 16, w3), lambda i, j: (i, 0, j, 0)),
        ],
        out_shape=[
            jax.ShapeDtypeStruct((b, s, w_a.shape[1]), BF16),
            jax.ShapeDtypeStruct((b, s, 2 * dm), BF16),
            jax.ShapeDtypeStruct((b, s, LANES), F32),
            jax.ShapeDtypeStruct((b, s, w3), BF16),
            jax.ShapeDtypeStruct((b, 4, s // 4, w3), BF16),
            jax.ShapeDtypeStruct((b, 16, s // 16, w3), BF16),
        ],
        compiler_params=_cparams(("parallel", "parallel")),
        name="in_proj",
    )(x, norm_w, w_a, w_gate, w_ab, *w_grp)


def _shifted(x, prev_row, next_row):
    t = x.shape[0]
    row = _iota((t, 1), 0)
    up = jnp.where(row == 0, prev_row, pltpu.roll(x, 1, 0))
    dn = jnp.where(row == t - 1, next_row, pltpu.roll(x, t - 1, 0))
    return up, dn


def _dn_prep_kernel(x_ref, xp_ref, xn_ref, cw_ref, ab_ref, alog_ref, dtb_ref, hm_ref, tri_ref, place_ref,
                    q_ref, k_ref, v_ref, cp_ref):
    j = pl.program_id(1)
    nj = pl.num_programs(1)
    x = x_ref[...].astype(F32)
    t = x.shape[0]
    prev_row = xp_ref[BF16_ROWS - 1:BF16_ROWS, :].astype(F32) * jnp.where(j > 0, 1.0, 0.0)
    next_row = xn_ref[0:1, :].astype(F32) * jnp.where(j < nj - 1, 1.0, 0.0)
    up, dn = _shifted(x, prev_row, next_row)
    cw = cw_ref[...]
    y = _silu(cw[0:1] * up + cw[1:2] * x + cw[2:3] * dn)
    q = y[:, 0:WIDTH]
    k = y[:, WIDTH:2 * WIDTH]
    hm = hm_ref[...]
    q_ref[...] = (q * lax.rsqrt(_head_meansq(q, hm) * HEAD_DIM + EPS)).astype(BF16)
    k_ref[...] = (k * lax.rsqrt(_head_meansq(k, hm) * HEAD_DIM + EPS)).astype(BF16)
    v_ref[...] = y[:, 2 * WIDTH:3 * WIDTH].astype(BF16)

    ab = ab_ref[...]
    a = ab[:, 0:16] + dtb_ref[...]
    softplus = jnp.maximum(a, 0.0) + jnp.log(1.0 + jnp.exp(-jnp.abs(a)))
    g = -jnp.exp(alog_ref[...]) * softplus
    beta = _sigmoid(ab[:, 16:32])
    lane = _iota((t, 16), 1)
    gc = jnp.where(lane < HEADS, _dot3(tri_ref[0], g), _dot3(tri_ref[1], g))
    acc = jnp.zeros((t, LANES), F32)
    for idx, piece in enumerate(_split3(gc) + _split3(beta)):
        acc = acc + _dot(piece, place_ref[idx])
    cp_ref[...] = acc.astype(BF16)


def _dn_prep_constants(tp):
    r = jnp.arange(tp)[:, None]
    c = jnp.arange(tp)[None, :]
    same = (r // CHUNK) == (c // CHUNK)
    tri = jnp.stack([same & (r >= c), same & (r <= c)]).astype(BF16)
    idx = jnp.arange(6)[:, None, None]
    place = (jnp.arange(LANES)[None, None, :] == jnp.arange(16)[None, :, None] + 16 * idx).astype(BF16)
    return tri, place


def _dn_prep(pa, conv_w, ab, alog, dtb, hm, tp=256):
    b, s, _ = pa.shape
    w3 = 3 * WIDTH
    nblk = s // BF16_ROWS
    per = tp // BF16_ROWS
    tri, place = _dn_prep_constants(tp)
    const2 = lambda i, j: (0, 0)
    const3 = lambda i, j: (0, 0, 0)
    return pl.pallas_call(
        _dn_prep_kernel,
        grid=(b, s // tp),
        in_specs=[
            pl.BlockSpec((None, tp, w3), lambda i, j: (i, j, 0)),
            pl.BlockSpec((None, BF16_ROWS, w3), lambda i, j: (i, jnp.maximum(j * per - 1, 0), 0)),
            pl.BlockSpec((None, BF16_ROWS, w3), lambda i, j: (i, jnp.minimum((j + 1) * per, nblk - 1), 0)),
            pl.BlockSpec((3, w3), lambda i, j: (0, 0)),
            pl.BlockSpec((None, tp, LANES), lambda i, j: (i, j, 0)),
            pl.BlockSpec((1, 16), const2),
            pl.BlockSpec((1, 16), const2),
            pl.BlockSpec((WIDTH, WIDTH), const2, pipeline_mode=pl.Buffered(1)),
            pl.BlockSpec((2, tp, tp), const3, pipeline_mode=pl.Buffered(1)),
            pl.BlockSpec((6, 16, LANES), const3, pipeline_mode=pl.Buffered(1)),
        ],
        out_specs=[pl.BlockSpec((None, tp, WIDTH), lambda i, j: (i, j, 0))] * 3
        + [pl.BlockSpec((None, tp, LANES), lambda i, j: (i, j, 0))],
        out_shape=[jax.ShapeDtypeStruct((b, s, WIDTH), BF16)] * 3
        + [jax.ShapeDtypeStruct((b, s, LANES), BF16)],
        compiler_params=_cparams(("parallel", "parallel")),
        name="dn_prep",
    )(pa, pa, pa, conv_w, ab, alog, dtb, hm, tri, place)


def _block_diag(x):
    blk = _div(_iota(x.shape, 1), HEAD_DIM)
    parts = [jnp.where(blk == a, x, 0.0) for a in range(4)]
    return jnp.concatenate(parts, axis=0).astype(BF16)


def _dn_scan_kernel(q_ref, k_ref, v_ref, cp_ref, e2_ref, o_ref,
                    s_scr, u_scr, l1_scr, l2_scr, al_scr):
    dirn = pl.program_id(1)
    seg = pl.program_id(2)
    nb = q_ref.shape[0]
    nchunk = q_ref.shape[1] // CHUNK
    fwd = dirn == 0

    @pl.when(seg == 0)
    def _():
        s_scr[...] = jnp.zeros_like(s_scr)

    row = _iota((CHUNK, QUAD), 0)
    col = _mod(_iota((CHUNK, QUAD), 1), HEAD_DIM)
    later = jnp.where(fwd, row, col)
    earlier = jnp.where(fwd, col, row)
    incl = later >= earlier
    strict = later > earlier
    diag = row == col
    eye = diag.astype(BF16)
    unroll = max(1, PHASE_A_PROBLEMS // (2 * nb))

    def phase_a(it, carry):
        probs = []
        for u in range(unroll):
            c = it * unroll + u
            rows = pl.ds(pl.multiple_of(c * CHUNK, CHUNK), CHUNK)
            for bi in range(nb):
                x = _dot(cp_ref[bi, rows, :], e2_ref[...])
                for qd in range(2):
                    lanes = slice(qd * QUAD, (qd + 1) * QUAD)
                    probs.append(dict(
                        c=c, slot=2 * bi + qd, gcol=x[:, qd * QUAD:(qd + 1) * QUAD],
                        beta=x[:, WIDTH + qd * QUAD:WIDTH + (qd + 1) * QUAD],
                        kq=k_ref[bi, rows, lanes], qq=q_ref[bi, rows, lanes], vq=v_ref[bi, rows, lanes]))
        for pr in probs:
            pr["kf"] = pr["kq"].astype(F32)
            pr["z"] = _dot_nt(jnp.concatenate([pr["kq"], pr["qq"], eye], axis=0),
                              _block_diag(pr["kf"]))
        for pr in probs:
            gcol, beta, z = pr["gcol"], pr["beta"], pr["z"]
            grow = jnp.sum(jnp.where(diag, gcol, 0.0), axis=0, keepdims=True)
            glast = jnp.where(fwd, gcol[CHUNK - 1:CHUNK, :], gcol[0:1, :])
            dm = jnp.exp(jnp.where(incl, gcol - grow, NEG_INF))
            eg = jnp.exp(gcol)
            a = jnp.where(strict, beta * z[0:CHUNK] * dm, 0.0)
            qkm = jnp.where(incl, z[CHUNK:2 * CHUNK] * dm, 0.0) * 0.125
            kdt = z[2 * CHUNK:3 * CHUNK] * jnp.exp(glast - grow)
            pr["vb"] = pr["vq"].astype(F32) * beta
            pr["kbg"] = pr["kf"] * (beta * eg)
            qdec = pr["qq"].astype(F32) * (eg * 0.125)
            c, slot = pr["c"], pr["slot"]
            l2_scr[slot, c] = jnp.concatenate([qkm, kdt], axis=0).astype(BF16)
            l1_scr[slot, c, CHUNK:2 * CHUNK, :] = qdec.astype(BF16)
            al_scr[slot, c] = jnp.broadcast_to(jnp.exp(glast), (8, QUAD))
            pr["a"] = a
        for pr in probs:
            pr["p"] = _dot(pr["a"].astype(BF16), _block_diag(pr["a"]))
            pr["n"] = -pr["a"]
        for _ in range(4):
            for pr in probs:
                n, p = pr["n"], pr["p"]
                zz = _dot(jnp.concatenate([n, p], axis=0).astype(BF16), _block_diag(p))
                pr["n"] = n + p + zz[0:CHUNK]
                pr["p"] = zz[CHUNK:2 * CHUNK]
        for pr in probs:
            n, p = pr["n"], pr["p"]
            pr["n"] = n + p + _dot(n.astype(BF16), _block_diag(p))
        for pr in probs:
            vb, kbg, c, slot = pr["vb"], pr["kbg"], pr["c"], pr["slot"]
            rhs = jnp.concatenate([_block_diag(vb), _block_diag(kbg)], axis=1)
            uw = _dot(pr["n"].astype(BF16), rhs)
            u_scr[slot, c] = vb + uw[:, 0:QUAD]
            l1_scr[slot, c, 0:CHUNK, :] = (kbg + uw[:, QUAD:2 * QUAD]).astype(BF16)
        return carry

    lax.fori_loop(0, nchunk // unroll, phase_a, 0)

    nslot = 2 * nb

    def phase_b(i, states):
        c = jnp.where(fwd, i, nchunk - 1 - i)
        r0 = pl.multiple_of(c * CHUNK, CHUNK)
        z1 = [_dot(l1_scr[sl, c], _block_diag(states[sl])) for sl in range(nslot)]
        vn = [u_scr[sl, c] - z1[sl][0:CHUNK] for sl in range(nslot)]
        z2 = [_dot(l2_scr[sl, c], _block_diag(vn[sl])) for sl in range(nslot)]
        new = []
        for sl in range(nslot):
            o = z1[sl][CHUNK:2 * CHUNK] + z2[sl][0:CHUNK]
            o_ref[sl // 2, pl.ds(r0, CHUNK), (sl % 2) * QUAD:(sl % 2 + 1) * QUAD] = o.astype(o_ref.dtype)
            new.append(al_scr[sl, c][0:1, :] * states[sl] + z2[sl][CHUNK:2 * CHUNK])
        return tuple(new)

    final = lax.fori_loop(0, nchunk, phase_b, tuple(s_scr[sl] for sl in range(nslot)))
    for sl in range(nslot):
        s_scr[sl] = final[sl]


def _dn_scan(q, k, v, cp, e2, nb=4, seg=512):
    b, s, _ = q.shape
    while b % nb:
        nb //= 2
    seg = min(seg, s)
    nseg = s // seg
    nchunk = seg // CHUNK

    def tok(i, d, j):
        return (i, jnp.where(d == 0, j, nseg - 1 - j), 0)

    return pl.pallas_call(
        _dn_scan_kernel,
        grid=(b // nb, 2, nseg),
        in_specs=[pl.BlockSpec((nb, seg, WIDTH), tok)] * 3
        + [pl.BlockSpec((nb, seg, LANES), tok),
           pl.BlockSpec((None, LANES, 2 * WIDTH), lambda i, d, j: (d, 0, 0))],
        out_specs=pl.BlockSpec((None, nb, seg, WIDTH),
                               lambda i, d, j: (d, i, jnp.where(d == 0, j, nseg - 1 - j), 0)),
        out_shape=jax.ShapeDtypeStruct((2, b, s, WIDTH), BF16),
        scratch_shapes=[
            pltpu.VMEM((2 * nb, CHUNK, QUAD), F32),
            pltpu.VMEM((2 * nb, nchunk, CHUNK, QUAD), F32),
            pltpu.VMEM((2 * nb, nchunk, 2 * CHUNK, QUAD), BF16),
            pltpu.VMEM((2 * nb, nchunk, 2 * CHUNK, QUAD), BF16),
            pltpu.VMEM((2 * nb, nchunk, 8, QUAD), F32),
        ],
        compiler_params=_cparams(("parallel", "arbitrary", "arbitrary")),
        name="dn_scan",
    )(q, k, v, cp, e2)


def _dn_expand_matrix():
    r = jnp.arange(LANES)[:, None]
    c = jnp.arange(2 * WIDTH)[None, :]
    piece, lane = r // 16, r % 16
    out = []
    for d in range(2):
        head = lane - 8 * d
        ok = (piece < 6) & (head >= 0) & (head < HEADS)
        ok = ok & ((c // WIDTH) == (piece // 3)) & (((c % WIDTH) // HEAD_DIM) == head)
        out.append(ok)
    return jnp.stack(out).astype(BF16)


QBLK = 2 * RADIUS


def _attn_kernel(q_ref, k_ref, kp_ref, kn_ref, v_ref, vp_ref, vn_ref, qw_ref, kw_ref, hm_ref, bias_ref,
                 o_ref, lse_ref, *, sub):
    t = pl.program_id(1)
    tq = q_ref.shape[0]
    nqb = tq // QBLK
    low = _mod(_iota((1, WIDTH), 1), LANES) < HEAD_DIM
    hm = hm_ref[...]
    q = q_ref[...].astype(F32)
    qn = q * lax.rsqrt(_dot((q * q).astype(BF16), hm) + EPS) * (qw_ref[...] * (HEAD_DIM ** -0.5 * LOG2E))
    qsel = (jnp.where(low, qn, 0.0).astype(BF16), jnp.where(low, 0.0, qn).astype(BF16))
    kx = jnp.concatenate([kp_ref[...], k_ref[...], kn_ref[...]], axis=0).astype(F32)
    kn = (kx * lax.rsqrt(_dot((kx * kx).astype(BF16), hm) + EPS) * kw_ref[...]).astype(BF16)
    vx = jnp.concatenate([vp_ref[...], v_ref[...], vn_ref[...]], axis=0).astype(F32)
    vsel = (jnp.where(low, vx, 1.0).astype(BF16), jnp.where(low, 1.0, vx).astype(BF16))
    c = _iota((QBLK, 2 * QBLK), 1)
    lane = _iota((1, LANES), 1)
    low_pair = lane < HEAD_DIM
    pairs = [slice((h // 2) * LANES, (h // 2 + 1) * LANES) for h in range(HEADS)]
    for qb in range(nqb):
        rows = slice(qb * QBLK, (qb + 1) * QBLK)
        krows = slice(qb * QBLK, qb * QBLK + 2 * QBLK)
        scores = [_dot_nt(qsel[h % 2][rows, pairs[h]], kn[krows, pairs[h]]) for h in range(HEADS)]
        edge = None
        if qb == 0 or qb == nqb - 1:
            kj = t * tq + (qb * QBLK - RADIUS) + c
            edge = jnp.where((kj >= 0) & (kj < sub), 0.0, NEG_INF)
        probs, maxes = [], []
        for h in range(HEADS):
            sc = scores[h] + bias_ref[h]
            if edge is not None:
                sc = sc + edge
            m = jnp.max(sc, axis=-1, keepdims=True)
            probs.append(jnp.exp2(sc - m).astype(BF16))
            maxes.append(m)
        outs = [_dot(probs[h], vsel[h % 2][krows, pairs[h]]) for h in range(HEADS)]
        lse_tile = jnp.zeros((QBLK, LANES), F32)
        for h0 in range(0, HEADS, 2):
            even, odd = outs[h0], outs[h0 + 1]
            num = jnp.where(low_pair, even, odd)
            den = pltpu.roll(jnp.where(low_pair, odd, even), HEAD_DIM, 1)
            o_ref[rows, pairs[h0]] = (num / den).astype(o_ref.dtype)
            lse_pair = jnp.where(low_pair, maxes[h0], maxes[h0 + 1]) * LN2 + jnp.log(den)
            keep = (lane == _lse_lane(h0)) | (lane == _lse_lane(h0 + 1))
            lse_tile = lse_tile + jnp.where(keep, lse_pair, 0.0)
        lse_ref[rows, :] = lse_tile


def _alibi_bias(dil, group):
    a = jnp.arange(QBLK)[:, None]
    c = jnp.arange(2 * QBLK)[None, :]
    rel = jnp.abs(a + RADIUS - c)
    slopes = 2.0 ** (-ALIBI_MAX * (group * HEADS + jnp.arange(1, HEADS + 1, dtype=F32)) / N_ATT_HEADS)
    bias = -(slopes * LOG2E)[:, None, None] * (rel * dil).astype(F32)[None]
    return jnp.where((rel <= RADIUS)[None], bias, NEG_INF)


def _attn(qkv, qw, kw, hm, *, dil, group):
    nseq, sub, _ = qkv.shape
    tq = min(512, sub)
    nt = sub // tq
    per = tq // RADIUS
    nblk = sub // RADIUS

    def main(lb):
        return pl.BlockSpec((None, tq, WIDTH), lambda i, j: (i, j, lb))

    def prev(lb):
        return pl.BlockSpec((None, RADIUS, WIDTH), lambda i, j: (i, jnp.maximum(j * per - 1, 0), lb))

    def nxt(lb):
        return pl.BlockSpec((None, RADIUS, WIDTH),
                            lambda i, j: (i, jnp.minimum((j + 1) * per, nblk - 1), lb))

    wspec = pl.BlockSpec((1, WIDTH), lambda i, j: (0, 0))
    hmspec = pl.BlockSpec((WIDTH, WIDTH), lambda i, j: (0, 0), pipeline_mode=pl.Buffered(1))
    bspec = pl.BlockSpec((HEADS, QBLK, 2 * QBLK), lambda i, j: (0, 0, 0), pipeline_mode=pl.Buffered(1))
    return pl.pallas_call(
        functools.partial(_attn_kernel, sub=sub),
        grid=(nseq, nt),
        in_specs=[main(0), main(1), prev(1), nxt(1), main(2), prev(2), nxt(2), wspec, wspec, hmspec, bspec],
        out_specs=[pl.BlockSpec((None, tq, WIDTH), lambda i, j: (i, j, 0)),
                   pl.BlockSpec((None, tq, LANES), lambda i, j: (i, j, 0))],
        out_shape=[jax.ShapeDtypeStruct((nseq, sub, WIDTH), BF16),
                   jax.ShapeDtypeStruct((nseq, sub, LANES), F32)],
        compiler_params=_cparams(("parallel", "parallel")),
        name=f"attn_g{group}",
    )(qkv, qkv, qkv, qkv, qkv, qkv, qkv, qw, kw, hm, _alibi_bias(dil, group))


def _merge_kernel(of_ref, ob_ref, z_ref, o0_ref, o1_ref, o2_ref, l0_ref, l1_ref, l2_ref,
                  gate_ref, x_ref, wdn_ref, wat_ref, wo_ref, dnw_ref, n2w_ref, hm_ref, pinv_ref, expand_ref,
                  x1_ref, h2_ref):
    tm = x_ref.shape[0]
    oa = of_ref[...].astype(F32) + ob_ref[...].astype(F32)
    z = z_ref[...].astype(F32)
    gated = oa * lax.rsqrt(_head_meansq(oa, hm_ref[...]) + EPS) * dnw_ref[...] * _silu(z)
    ya = _dot(gated.astype(BF16), wdn_ref[...])
    outs = [o0_ref[...].astype(F32)]
    lses = [l0_ref[...]]
    for idx, (d, oref, lref) in enumerate(((4, o1_ref, l1_ref), (16, o2_ref, l2_ref))):
        pinv = pinv_ref[idx]
        outs.append(_dot(pinv, jnp.concatenate([oref[r] for r in range(d)], axis=0)))
        lses.append(_dot3(pinv, jnp.concatenate([lref[r] for r in range(d)], axis=0)))
    m = jnp.maximum(jnp.maximum(lses[0], lses[1]), lses[2])
    es = [jnp.exp(l - m) for l in lses]
    den = es[0] + es[1] + es[2]
    expand = expand_ref[...]
    ob = jnp.zeros((tm, WIDTH), F32)
    for e, o in zip(es, outs):
        hi, mid, _ = _split3(e / den)
        ob = ob + (_dot(hi, expand) + _dot(mid, expand)) * o
    yb = _dot(ob.astype(BF16), wat_ref[...])
    g = gate_ref[...].astype(F32)
    dm = x_ref.shape[1]
    mixed = _sigmoid(g[:, 0:dm]) * ya + _sigmoid(g[:, dm:2 * dm]) * yb
    x1 = x_ref[...] + _dot(mixed.astype(BF16), wo_ref[...])
    x1_ref[...] = x1
    ms = jnp.mean(x1 * x1, axis=-1, keepdims=True)
    h2_ref[...] = (x1 * lax.rsqrt(ms + EPS) * n2w_ref[...]).astype(BF16)


def _merge(o_dn, pa, attn_o, attn_l, gates, x, wdn, wat, wo, dnw, n2w, hm, tm=256):
    b, s, dm = x.shape
    tok = lambda i, j: (i, j, 0)
    const = lambda i, j: (0, 0)
    pinv = jnp.stack([_perm_matrix(tm, 4, inverse=True), _perm_matrix(tm, 16, inverse=True)])
    in_specs = [
        pl.BlockSpec((None, None, tm, WIDTH), lambda i, j: (0, i, j, 0)),
        pl.BlockSpec((None, None, tm, WIDTH), lambda i, j: (1, i, j, 0)),
        pl.BlockSpec((None, tm, WIDTH), lambda i, j: (i, j, 3)),
        pl.BlockSpec((None, tm, WIDTH), tok),
        pl.BlockSpec((None, 4, tm // 4, WIDTH), lambda i, j: (i, 0, j, 0)),
        pl.BlockSpec((None, 16, tm // 16, WIDTH), lambda i, j: (i, 0, j, 0)),
        pl.BlockSpec((None, tm, LANES), tok),
        pl.BlockSpec((None, 4, tm // 4, LANES), lambda i, j: (i, 0, j, 0)),
        pl.BlockSpec((None, 16, tm // 16, LANES), lambda i, j: (i, 0, j, 0)),
        pl.BlockSpec((None, tm, 2 * dm), tok),
        pl.BlockSpec((None, tm, dm), tok),
        pl.BlockSpec((WIDTH, dm), const),
        pl.BlockSpec((WIDTH, dm), const),
        pl.BlockSpec((dm, dm), const),
        pl.BlockSpec((1, WIDTH), const),
        pl.BlockSpec((1, dm), const),
        pl.BlockSpec((WIDTH, WIDTH), const, pipeline_mode=pl.Buffered(1)),
        pl.BlockSpec((2, tm, tm), lambda i, j: (0, 0, 0), pipeline_mode=pl.Buffered(1)),
        pl.BlockSpec((LANES, WIDTH), const, pipeline_mode=pl.Buffered(1)),
    ]
    return pl.pallas_call(
        _merge_kernel,
        grid=(b, s // tm),
        in_specs=in_specs,
        out_specs=[pl.BlockSpec((None, tm, dm), tok), pl.BlockSpec((None, tm, dm), tok)],
        out_shape=[jax.ShapeDtypeStruct((b, s, dm), F32), jax.ShapeDtypeStruct((b, s, dm), BF16)],
        compiler_params=_cparams(("parallel", "parallel")),
        name="merge",
    )(o_dn, o_dn, pa, attn_o[0], attn_o[1], attn_o[2], attn_l[0], attn_l[1], attn_l[2],
      gates, x, wdn, wat, wo, dnw, n2w, hm, pinv, _head_expand())


def _ffn_kernel(h_ref, hp_ref, hn_ref, wg_ref, wu_ref, cg_ref, cu_ref, wd_ref, x1_ref, o_ref,
                acc_ref, ug_ref, uu_ref, *, tiles_per_seq):
    i = pl.program_id(0)
    j = pl.program_id(1)
    tm = h_ref.shape[0]
    pos = _mod(i, tiles_per_seq)
    hp = hp_ref[...] * jnp.where(pos > 0, 1.0, 0.0).astype(BF16)
    hn = hn_ref[...] * jnp.where(pos < tiles_per_seq - 1, 1.0, 0.0).astype(BF16)
    lhs = jnp.concatenate([hp, h_ref[...], hn], axis=0)

    def conv_branch(w_ref, c_ref, ue_ref):
        ue_ref[...] = _dot(lhs, w_ref[...])
        cw = c_ref[...]
        return (cw[0:1] * ue_ref[BF16_ROWS - 1:BF16_ROWS - 1 + tm, :]
                + cw[1:2] * ue_ref[BF16_ROWS:BF16_ROWS + tm, :]
                + cw[2:3] * ue_ref[BF16_ROWS + 1:BF16_ROWS + 1 + tm, :])

    act = _silu(conv_branch(wg_ref, cg_ref, ug_ref)) * conv_branch(wu_ref, cu_ref, uu_ref)
    part = _dot(act.astype(BF16), wd_ref[...])

    @pl.when(j == 0)
    def _():
        acc_ref[...] = part

    @pl.when(j > 0)
    def _():
        acc_ref[...] += part

    @pl.when(j == pl.num_programs(1) - 1)
    def _():
        o_ref[...] = x1_ref[...] + acc_ref[...]


def _ffn(h2, x1, w_up, conv_w, w_down, seq, tm=512, tf=1408):
    n, dm = h2.shape
    dff = w_down.shape[0]
    nf = dff // tf
    per = tm // BF16_ROWS
    nblk = n // BF16_ROWS
    return pl.pallas_call(
        functools.partial(_ffn_kernel, tiles_per_seq=seq // tm),
        grid=(n // tm, nf),
        in_specs=[
            pl.BlockSpec((tm, dm), lambda i, j: (i, 0)),
            pl.BlockSpec((BF16_ROWS, dm), lambda i, j: (jnp.maximum(i * per - 1, 0), 0)),
            pl.BlockSpec((BF16_ROWS, dm), lambda i, j: (jnp.minimum((i + 1) * per, nblk - 1), 0)),
            pl.BlockSpec((dm, tf), lambda i, j: (0, j)),
            pl.BlockSpec((dm, tf), lambda i, j: (0, nf + j)),
            pl.BlockSpec((3, tf), lambda i, j: (0, j)),
            pl.BlockSpec((3, tf), lambda i, j: (0, nf + j)),
            pl.BlockSpec((tf, dm), lambda i, j: (j, 0)),
            pl.BlockSpec((tm, dm), lambda i, j: (i, 0)),
        ],
        out_specs=pl.BlockSpec((tm, dm), lambda i, j: (i, 0)),
        out_shape=jax.ShapeDtypeStruct((n, dm), F32),
        scratch_shapes=[pltpu.VMEM((tm, dm), F32),
                        pltpu.VMEM((tm + 2 * BF16_ROWS, tf), F32),
                        pltpu.VMEM((tm + 2 * BF16_ROWS, tf), F32)],
        compiler_params=_cparams(("parallel", "arbitrary")),
        name="ffn",
    )(h2, h2, h2, w_up, w_up, conv_w, conv_w, w_down, x1)


def _layer(x, norm1_w, w_in, dn_conv_w, dn_a_log, dn_dt_bias, dn_out_norm_w, attn_q_norm_w,
           attn_k_norm_w, w_dn_out, w_attn_out, w_o, norm2_w, w_ffn_up, ffn_conv_w, w_ffn_down):
    b, s, dm = x.shape
    n = b * s
    w3 = 3 * WIDTH
    c_z = w3
    c_ab = c_z + WIDTH
    c_q = c_ab + 32
    c_k = c_q + w3
    c_v = c_k + w3
    c_gate = c_v + w3
    wb = w_in.astype(BF16)
    w_a = wb[:, 0:c_ab]
    w_ab = jnp.pad(wb[:, c_ab:c_q], ((0, 0), (0, LANES - 32)))
    w_grp = [jnp.concatenate([wb[:, c0 + g * WIDTH:c0 + (g + 1) * WIDTH] for c0 in (c_q, c_k, c_v)], axis=1)
             for g in range(3)]
    w_gate = wb[:, c_gate:c_gate + 2 * dm]

    hm = _head_mean_matrix()

    pa, gates, ab, qkv0, qkv1, qkv2 = _in_proj(x, norm1_w.reshape(1, dm), w_a, w_gate, w_ab, w_grp)

    q, k, v, cp = _dn_prep(pa, dn_conv_w, ab, dn_a_log.reshape(1, 16), dn_dt_bias.reshape(1, 16), hm)
    o_dn = _dn_scan(q, k, v, cp, _dn_expand_matrix())

    qw = jnp.tile(attn_q_norm_w, HEADS).reshape(1, WIDTH)
    kw = jnp.tile(attn_k_norm_w, HEADS).reshape(1, WIDTH)
    attn_o, attn_l = [], []
    for g, (d, qkv) in enumerate(zip(DILATIONS, (qkv0, qkv1, qkv2))):
        o, l = _attn(qkv.reshape(b * d, s // d, w3), qw, kw, hm, dil=d, group=g)
        shape = (b, s) if d == 1 else (b, d, s // d)
        attn_o.append(o.reshape(shape + (WIDTH,)))
        attn_l.append(l.reshape(shape + (LANES,)))

    x1, h2 = _merge(o_dn, pa, attn_o, attn_l, gates, x, w_dn_out.astype(BF16), w_attn_out.astype(BF16),
                    w_o.astype(BF16), jnp.tile(dn_out_norm_w, HEADS).reshape(1, WIDTH),
                    norm2_w.reshape(1, dm), hm)
    out = _ffn(h2.reshape(n, dm), x1.reshape(n, dm), w_ffn_up.astype(BF16), ffn_conv_w,
               w_ffn_down.astype(BF16), s)
    return out.reshape(b, s, dm)


def kernel(x, norm1_w, w_in, dn_conv_w, dn_a_log, dn_dt_bias, dn_out_norm_w, attn_q_norm_w, attn_k_norm_w, w_dn_out, w_attn_out, w_o, norm2_w, w_ffn_up, ffn_conv_w, w_ffn_down):
    for layer in range(norm1_w.shape[0]):
        x = _layer(x, norm1_w[layer], w_in[layer], dn_conv_w[layer], dn_a_log[layer], dn_dt_bias[layer],
                   dn_out_norm_w[layer], attn_q_norm_w[layer], attn_k_norm_w[layer], w_dn_out[layer],
                   w_attn_out[layer], w_o[layer], norm2_w[layer], w_ffn_up[layer], ffn_conv_w[layer],
                   w_ffn_down[layer])
    return x
```

```python
import functools

import jax
import jax.numpy as jnp
from jax import lax
from jax.experimental import pallas as pl
from jax.experimental.pallas import tpu as pltpu

F32 = jnp.float32
BF16 = jnp.bfloat16

EPS = 1e-6
NEG_INF = -1e30
LOG2E = 1.4426950408889634
LN2 = 0.6931471805599453

HEAD_DIM = 64
HEADS = 8
WIDTH = HEADS * HEAD_DIM
CHUNK = 64
RADIUS = 64
DILATIONS = (1, 4, 16)
ALIBI_MAX = 8.0
N_ATT_HEADS = 24
QUAD = 4 * HEAD_DIM
PHASE_A_PROBLEMS = 16
BF16_ROWS = 16
LANES = 128
V7X_VMEM_LIMIT = 56 * 1024 * 1024


def _cparams(sem):
    return pltpu.CompilerParams(dimension_semantics=sem, vmem_limit_bytes=V7X_VMEM_LIMIT)


def _dot(a, b):
    return jnp.dot(a, b, preferred_element_type=F32)


def _dot_nt(a, b):
    return lax.dot_general(a, b, (((1,), (1,)), ((), ())), preferred_element_type=F32)


def _split3(x):
    hi = x.astype(BF16)
    r1 = x - hi.astype(F32)
    mid = r1.astype(BF16)
    lo = (r1 - mid.astype(F32)).astype(BF16)
    return hi, mid, lo


def _dot3(a, x):
    hi, mid, lo = _split3(x)
    return _dot(a, hi) + _dot(a, mid) + _dot(a, lo)


def _log2(n):
    assert n > 0 and n & (n - 1) == 0, n
    return n.bit_length() - 1


def _div(x, n):
    return lax.shift_right_logical(x, _log2(n))


def _mod(x, n):
    assert n & (n - 1) == 0, n
    return x & (n - 1)


def _iota(shape, dim):
    return lax.broadcasted_iota(jnp.int32, shape, dim)


def _sigmoid(x):
    return 0.5 * jnp.tanh(0.5 * x) + 0.5


def _silu(x):
    return x * _sigmoid(x)


def _head_mean_matrix():
    r = jnp.arange(WIDTH)[:, None] // HEAD_DIM
    c = jnp.arange(WIDTH)[None, :] // HEAD_DIM
    return jnp.where(r == c, 1.0 / HEAD_DIM, 0.0).astype(BF16)


def _head_meansq(x, hm):
    return _dot((x * x).astype(BF16), hm)


def _lse_lane(h):
    return h if h % 2 == 0 else HEAD_DIM + h


def _head_expand():
    r = _iota((LANES, WIDTH), 0)
    head = _div(_iota((LANES, WIDTH), 1), HEAD_DIM)
    lane = jnp.where(_mod(head, 2) == 0, head, head + HEAD_DIM)
    return (r == lane).astype(BF16)


def _perm_matrix(tm, d, inverse=False):
    run = tm // d
    row = _iota((tm, tm), 0)
    col = _iota((tm, tm), 1)
    if inverse:
        src = _mod(row, d) * run + _div(row, d)
    else:
        src = _mod(row, run) * d + _div(row, run)
    return (col == src).astype(BF16)


def _in_proj_kernel(x_ref, nw_ref, wa_ref, wgate_ref, wab_ref, wg0_ref, wg1_ref, wg2_ref,
                    pa_ref, gate_ref, ab_ref, q0_ref, q1_ref, q2_ref):
    x = x_ref[...]
    ms = jnp.mean(x * x, axis=-1, keepdims=True)
    h = (x * lax.rsqrt(ms + EPS) * nw_ref[...]).astype(BF16)
    tm = x.shape[0]
    h4 = _dot(_perm_matrix(tm, 4), h).astype(BF16)
    h16 = _dot(_perm_matrix(tm, 16), h).astype(BF16)
    pa_ref[...] = _dot(h, wa_ref[...]).astype(BF16)
    gate_ref[...] = _dot(h, wgate_ref[...]).astype(BF16)
    q0_ref[...] = _dot(h, wg0_ref[...]).astype(BF16)
    ab_ref[...] = _dot(h, wab_ref[...])
    for d, hp, w_ref, ref in ((4, h4, wg1_ref, q1_ref), (16, h16, wg2_ref, q2_ref)):
        run = tm // d
        y = _dot(hp, w_ref[...]).astype(BF16)
        for r in range(d):
            ref[r] = y[r * run:(r + 1) * run]


def _in_proj(x, norm_w, w_a, w_gate, w_ab, w_grp, tm=256):
    b, s, dm = x.shape
    w3 = 3 * WIDTH
    tok = lambda i, j: (i, j, 0)

    def resident(shape):
        return pl.BlockSpec(shape, lambda i, j: (0, 0), pipeline_mode=pl.Buffered(1))

    return pl.pallas_call(
        _in_proj_kernel,
        grid=(b, s // tm),
        in_specs=[pl.BlockSpec((None, tm, dm), tok), resident((1, dm)), resident(w_a.shape),
                  resident(w_gate.shape), resident(w_ab.shape)] + [resident((dm, w3))] * 3,
        out_specs=[
            pl.BlockSpec((None, tm, w_a.shape[1]), tok),
            pl.BlockSpec((None, tm, 2 * dm), tok),
            pl.BlockSpec((None, tm, LANES), tok),
            pl.BlockSpec((None, tm, w3), tok),
            pl.BlockSpec((None, 4, tm // 4, w3), lambda i, j: (i, 0, j, 0)),
            pl.BlockSpec((None, 16, tm // 16, w3), lambda i, j: (i, 0, j, 0)),
        ],
        out_shape=[
            jax.ShapeDtypeStruct((b, s, w_a.shape[1]), BF16),
            jax.ShapeDtypeStruct((b, s, 2 * dm), BF16),
            jax.ShapeDtypeStruct((b, s, LANES), F32),
            jax.ShapeDtypeStruct((b, s, w3), BF16),
            jax.ShapeDtypeStruct((b, 4, s // 4, w3), BF16),
            jax.ShapeDtypeStruct((b, 16, s // 16, w3), BF16),
        ],
        compiler_params=_cparams(("parallel", "parallel")),
        name="in_proj",
    )(x, norm_w, w_a, w_gate, w_ab, *w_grp)


def _shifted(x, prev_row, next_row):
    t = x.shape[0]
    row = _iota((t, 1), 0)
    up = jnp.where(row == 0, prev_row, pltpu.roll(x, 1, 0))
    dn = jnp.where(row == t - 1, next_row, pltpu.roll(x, t - 1, 0))
    return up, dn


def _dn_prep_kernel(x_ref, xp_ref, xn_ref, cw_ref, ab_ref, alog_ref, dtb_ref, hm_ref, tri_ref, place_ref,
                    q_ref, k_ref, v_ref, cp_ref):
    j = pl.program_id(1)
    nj = pl.num_programs(1)
    x = x_ref[...].astype(F32)
    t = x.shape[0]
    prev_row = xp_ref[BF16_ROWS - 1:BF16_ROWS, :].astype(F32) * jnp.where(j > 0, 1.0, 0.0)
    next_row = xn_ref[0:1, :].astype(F32) * jnp.where(j < nj - 1, 1.0, 0.0)
    up, dn = _shifted(x, prev_row, next_row)
    cw = cw_ref[...]
    y = _silu(cw[0:1] * up + cw[1:2] * x + cw[2:3] * dn)
    q = y[:, 0:WIDTH]
    k = y[:, WIDTH:2 * WIDTH]
    hm = hm_ref[...]
    q_ref[...] = (q * lax.rsqrt(_head_meansq(q, hm) * HEAD_DIM + EPS)).astype(BF16)
    k_ref[...] = (k * lax.rsqrt(_head_meansq(k, hm) * HEAD_DIM + EPS)).astype(BF16)
    v_ref[...] = y[:, 2 * WIDTH:3 * WIDTH].astype(BF16)

    ab = ab_ref[...]
    a = ab[:, 0:16] + dtb_ref[...]
    softplus = jnp.maximum(a, 0.0) + jnp.log(1.0 + jnp.exp(-jnp.abs(a)))
    g = -jnp.exp(alog_ref[...]) * softplus
    beta = _sigmoid(ab[:, 16:32])
    lane = _iota((t, 16), 1)
    gc = jnp.where(lane < HEADS, _dot3(tri_ref[0], g), _dot3(tri_ref[1], g))
    acc = jnp.zeros((t, LANES), F32)
    for idx, piece in enumerate(_split3(gc) + _split3(beta)):
        acc = acc + _dot(piece, place_ref[idx])
    cp_ref[...] = acc.astype(BF16)


def _dn_prep_constants(tp):
    r = jnp.arange(tp)[:, None]
    c = jnp.arange(tp)[None, :]
    same = (r // CHUNK) == (c // CHUNK)
    tri = jnp.stack([same & (r >= c), same & (r <= c)]).astype(BF16)
    idx = jnp.arange(6)[:, None, None]
    place = (jnp.arange(LANES)[None, None, :] == jnp.arange(16)[None, :, None] + 16 * idx).astype(BF16)
    return tri, place


def _dn_prep(pa, conv_w, ab, alog, dtb, hm, tp=256):
    b, s, _ = pa.shape
    w3 = 3 * WIDTH
    nblk = s // BF16_ROWS
    per = tp // BF16_ROWS
    tri, place = _dn_prep_constants(tp)
    const2 = lambda i, j: (0, 0)
    const3 = lambda i, j: (0, 0, 0)
    return pl.pallas_call(
        _dn_prep_kernel,
        grid=(b, s // tp),
        in_specs=[
            pl.BlockSpec((None, tp, w3), lambda i, j: (i, j, 0)),
            pl.BlockSpec((None, BF16_ROWS, w3), lambda i, j: (i, jnp.maximum(j * per - 1, 0), 0)),
            pl.BlockSpec((None, BF16_ROWS, w3), lambda i, j: (i, jnp.minimum((j + 1) * per, nblk - 1), 0)),
            pl.BlockSpec((3, w3), lambda i, j: (0, 0)),
            pl.BlockSpec((None, tp, LANES), lambda i, j: (i, j, 0)),
            pl.BlockSpec((1, 16), const2),
            pl.BlockSpec((1, 16), const2),
            pl.BlockSpec((WIDTH, WIDTH), const2, pipeline_mode=pl.Buffered(1)),
            pl.BlockSpec((2, tp, tp), const3, pipeline_mode=pl.Buffered(1)),
            pl.BlockSpec((6, 16, LANES), const3, pipeline_mode=pl.Buffered(1)),
        ],
        out_specs=[pl.BlockSpec((None, tp, WIDTH), lambda i, j: (i, j, 0))] * 3
        + [pl.BlockSpec((None, tp, LANES), lambda i, j: (i, j, 0))],
        out_shape=[jax.ShapeDtypeStruct((b, s, WIDTH), BF16)] * 3
        + [jax.ShapeDtypeStruct((b, s, LANES), BF16)],
        compiler_params=_cparams(("parallel", "parallel")),
        name="dn_prep",
    )(pa, pa, pa, conv_w, ab, alog, dtb, hm, tri, place)


def _block_diag(x):
    blk = _div(_iota(x.shape, 1), HEAD_DIM)
    parts = [jnp.where(blk == a, x, 0.0) for a in range(4)]
    return jnp.concatenate(parts, axis=0).astype(BF16)


def _dn_scan_kernel(q_ref, k_ref, v_ref, cp_ref, e2_ref, o_ref,
                    s_scr, u_scr, l1_scr, l2_scr, al_scr):
    dirn = pl.program_id(1)
    seg = pl.program_id(2)
    nb = q_ref.shape[0]
    nchunk = q_ref.shape[1] // CHUNK
    fwd = dirn == 0

    @pl.when(seg == 0)
    def _():
        s_scr[...] = jnp.zeros_like(s_scr)

    row = _iota((CHUNK, QUAD), 0)
    col = _mod(_iota((CHUNK, QUAD), 1), HEAD_DIM)
    later = jnp.where(fwd, row, col)
    earlier = jnp.where(fwd, col, row)
    incl = later >= earlier
    strict = later > earlier
    diag = row == col
    eye = diag.astype(BF16)
    unroll = max(1, PHASE_A_PROBLEMS // (2 * nb))

    def phase_a(it, carry):
        probs = []
        for u in range(unroll):
            c = it * unroll + u
            rows = pl.ds(pl.multiple_of(c * CHUNK, CHUNK), CHUNK)
            x_all = _dot(jnp.concatenate([cp_ref[bi, rows, :] for bi in range(nb)], axis=0), e2_ref[...])
            for bi in range(nb):
                x = x_all[bi * CHUNK:(bi + 1) * CHUNK]
                for qd in range(2):
                    lanes = slice(qd * QUAD, (qd + 1) * QUAD)
                    probs.append(dict(
                        c=c, slot=2 * bi + qd, gcol=x[:, qd * QUAD:(qd + 1) * QUAD],
                        beta=x[:, WIDTH + qd * QUAD:WIDTH + (qd + 1) * QUAD],
                        kq=k_ref[bi, rows, lanes], qq=q_ref[bi, rows, lanes], vq=v_ref[bi, rows, lanes]))
        for pr in probs:
            pr["kf"] = pr["kq"].astype(F32)
            pr["z"] = _dot_nt(jnp.concatenate([pr["kq"], pr["qq"], eye], axis=0),
                              _block_diag(pr["kf"]))
        for pr in probs:
            gcol, beta, z = pr["gcol"], pr["beta"], pr["z"]
            grow = jnp.sum(jnp.where(diag, gcol, 0.0), axis=0, keepdims=True)
            glast = jnp.where(fwd, gcol[CHUNK - 1:CHUNK, :], gcol[0:1, :])
            dm = jnp.exp(jnp.where(incl, gcol - grow, NEG_INF))
            eg = jnp.exp(gcol)
            a = jnp.where(strict, beta * z[0:CHUNK] * dm, 0.0)
            qkm = jnp.where(incl, z[CHUNK:2 * CHUNK] * dm, 0.0) * 0.125
            kdt = z[2 * CHUNK:3 * CHUNK] * jnp.exp(glast - grow)
            pr["vb"] = pr["vq"].astype(F32) * beta
            pr["kbg"] = pr["kf"] * (beta * eg)
            qdec = pr["qq"].astype(F32) * (eg * 0.125)
            c, slot = pr["c"], pr["slot"]
            l2_scr[slot, c] = jnp.concatenate([qkm, kdt], axis=0).astype(BF16)
            l1_scr[slot, c, CHUNK:2 * CHUNK, :] = qdec.astype(BF16)
            al_scr[slot, c] = jnp.broadcast_to(jnp.exp(glast), (8, QUAD))
            pr["a"] = a
        for pr in probs:
            pr["p"] = _dot(pr["a"].astype(BF16), _block_diag(pr["a"]))
            pr["n"] = -pr["a"]
        for _ in range(4):
            for pr in probs:
                n, p = pr["n"], pr["p"]
                zz = _dot(jnp.concatenate([n, p], axis=0).astype(BF16), _block_diag(p))
                pr["n"] = n + p + zz[0:CHUNK]
                pr["p"] = zz[CHUNK:2 * CHUNK]
        for pr in probs:
            n, p = pr["n"], pr["p"]
            pr["n"] = n + p + _dot(n.astype(BF16), _block_diag(p))
        for pr in probs:
            vb, kbg, c, slot = pr["vb"], pr["kbg"], pr["c"], pr["slot"]
            rhs = jnp.concatenate([_block_diag(vb), _block_diag(kbg)], axis=1)
            uw = _dot(pr["n"].astype(BF16), rhs)
            u_scr[slot, c] = vb + uw[:, 0:QUAD]
            l1_scr[slot, c, 0:CHUNK, :] = (kbg + uw[:, QUAD:2 * QUAD]).astype(BF16)
        return carry

    lax.fori_loop(0, nchunk // unroll, phase_a, 0)

    nslot = 2 * nb

    def phase_b(i, states):
        c = jnp.where(fwd, i, nchunk - 1 - i)
        r0 = pl.multiple_of(c * CHUNK, CHUNK)
        z1 = [_dot(l1_scr[sl, c], _block_diag(states[sl])) for sl in range(nslot)]
        vn = [u_scr[sl, c] - z1[sl][0:CHUNK] for sl in range(nslot)]
        z2 = [_dot(l2_scr[sl, c], _block_diag(vn[sl])) for sl in range(nslot)]
        new = []
        for sl in range(nslot):
            o = z1[sl][CHUNK:2 * CHUNK] + z2[sl][0:CHUNK]
            o_ref[sl // 2, pl.ds(r0, CHUNK), (sl % 2) * QUAD:(sl % 2 + 1) * QUAD] = o.astype(o_ref.dtype)
            new.append(al_scr[sl, c][0:1, :] * states[sl] + z2[sl][CHUNK:2 * CHUNK])
        return tuple(new)

    final = lax.fori_loop(0, nchunk, phase_b, tuple(s_scr[sl] for sl in range(nslot)))
    for sl in range(nslot):
        s_scr[sl] = final[sl]


def _dn_scan(q, k, v, cp, e2, nb=4, seg=512):
    b, s, _ = q.shape
    while b % nb:
        nb //= 2
    seg = min(seg, s)
    nseg = s // seg
    nchunk = seg // CHUNK

    def tok(i, d, j):
        return (i, jnp.where(d == 0, j, nseg - 1 - j), 0)

    return pl.pallas_call(
        _dn_scan_kernel,
        grid=(b // nb, 2, nseg),
        in_specs=[pl.BlockSpec((nb, seg, WIDTH), tok)] * 3
        + [pl.BlockSpec((nb, seg, LANES), tok),
           pl.BlockSpec((None, LANES, 2 * WIDTH), lambda i, d, j: (d, 0, 0))],
        out_specs=pl.BlockSpec((None, nb, seg, WIDTH),
                               lambda i, d, j: (d, i, jnp.where(d == 0, j, nseg - 1 - j), 0)),
        out_shape=jax.ShapeDtypeStruct((2, b, s, WIDTH), BF16),
        scratch_shapes=[
            pltpu.VMEM((2 * nb, CHUNK, QUAD), F32),
            pltpu.VMEM((2 * nb, nchunk, CHUNK, QUAD), F32),
            pltpu.VMEM((2 * nb, nchunk, 2 * CHUNK, QUAD), BF16),
            pltpu.VMEM((2 * nb, nchunk, 2 * CHUNK, QUAD), BF16),
            pltpu.VMEM((2 * nb, nchunk, 8, QUAD), F32),
        ],
        compiler_params=_cparams(("parallel", "arbitrary", "arbitrary")),
        name="dn_scan",
    )(q, k, v, cp, e2)


def _dn_expand_matrix():
    r = jnp.arange(LANES)[:, None]
    c = jnp.arange(2 * WIDTH)[None, :]
    piece, lane = r // 16, r % 16
    out = []
    for d in range(2):
        head = lane - 8 * d
        ok = (piece < 6) & (head >= 0) & (head < HEADS)
        ok = ok & ((c // WIDTH) == (piece // 3)) & (((c % WIDTH) // HEAD_DIM) == head)
        out.append(ok)
    return jnp.stack(out).astype(BF16)


QBLK = 2 * RADIUS


def _attn_kernel(q_ref, k_ref, kp_ref, kn_ref, v_ref, vp_ref, vn_ref, qw_ref, kw_ref, hm_ref, bias_ref,
                 o_ref, lse_ref, *, sub):
    t = pl.program_id(1)
    tq = q_ref.shape[0]
    nqb = tq // QBLK
    low = _mod(_iota((1, WIDTH), 1), LANES) < HEAD_DIM
    hm = hm_ref[...]
    q = q_ref[...].astype(F32)
    qn = q * lax.rsqrt(_dot((q * q).astype(BF16), hm) + EPS) * (qw_ref[...] * (HEAD_DIM ** -0.5 * LOG2E))
    qn = qn.astype(BF16)
    zero = jnp.zeros((), BF16)
    one = jnp.ones((), BF16)
    qsel = (jnp.where(low, qn, zero), jnp.where(low, zero, qn))
    kx = jnp.concatenate([kp_ref[...], k_ref[...], kn_ref[...]], axis=0).astype(F32)
    kn = (kx * lax.rsqrt(_dot((kx * kx).astype(BF16), hm) + EPS) * kw_ref[...]).astype(BF16)
    vx = jnp.concatenate([vp_ref[...], v_ref[...], vn_ref[...]], axis=0)
    vsel = (jnp.where(low, vx, one), jnp.where(low, one, vx))
    c = _iota((QBLK, 2 * QBLK), 1)
    lane = _iota((1, LANES), 1)
    low_pair = lane < HEAD_DIM
    pairs = [slice((h // 2) * LANES, (h // 2 + 1) * LANES) for h in range(HEADS)]
    for qb in range(nqb):
        rows = slice(qb * QBLK, (qb + 1) * QBLK)
        krows = slice(qb * QBLK, qb * QBLK + 2 * QBLK)
        scores = [_dot_nt(qsel[h % 2][rows, pairs[h]], kn[krows, pairs[h]]) for h in range(HEADS)]
        edge = None
        if qb == 0 or qb == nqb - 1:
            kj = t * tq + (qb * QBLK - RADIUS) + c
            edge = jnp.where((kj >= 0) & (kj < sub), 0.0, NEG_INF)
        probs, maxes = [], []
        for h in range(HEADS):
            sc = scores[h] + bias_ref[h]
            if edge is not None:
                sc = sc + edge
            m = jnp.max(sc, axis=-1, keepdims=True)
            probs.append(jnp.exp2(sc - m).astype(BF16))
            maxes.append(m)
        outs = [_dot(probs[h], vsel[h % 2][krows, pairs[h]]) for h in range(HEADS)]
        lse_tile = jnp.zeros((QBLK, LANES), F32)
        for h0 in range(0, HEADS, 2):
            even, odd = outs[h0], outs[h0 + 1]
            num = jnp.where(low_pair, even, odd)
            den = pltpu.roll(jnp.where(low_pair, odd, even), HEAD_DIM, 1)
            o_ref[rows, pairs[h0]] = (num / den).astype(o_ref.dtype)
            lse_pair = jnp.where(low_pair, maxes[h0], maxes[h0 + 1]) * LN2 + jnp.log(den)
            keep = (lane == _lse_lane(h0)) | (lane == _lse_lane(h0 + 1))
            lse_tile = lse_tile + jnp.where(keep, lse_pair, 0.0)
        lse_ref[rows, :] = lse_tile


def _alibi_bias(dil, group):
    a = jnp.arange(QBLK)[:, None]
    c = jnp.arange(2 * QBLK)[None, :]
    rel = jnp.abs(a + RADIUS - c)
    slopes = 2.0 ** (-ALIBI_MAX * (group * HEADS + jnp.arange(1, HEADS + 1, dtype=F32)) / N_ATT_HEADS)
    bias = -(slopes * LOG2E)[:, None, None] * (rel * dil).astype(F32)[None]
    return jnp.where((rel <= RADIUS)[None], bias, NEG_INF)


def _attn(qkv, qw, kw, hm, *, dil, group):
    nseq, sub, _ = qkv.shape
    tq = min(512, sub)
    nt = sub // tq
    per = tq // RADIUS
    nblk = sub // RADIUS

    def main(lb):
        return pl.BlockSpec((None, tq, WIDTH), lambda i, j: (i, j, lb))

    def prev(lb):
        return pl.BlockSpec((None, RADIUS, WIDTH), lambda i, j: (i, jnp.maximum(j * per - 1, 0), lb))

    def nxt(lb):
        return pl.BlockSpec((None, RADIUS, WIDTH),
                            lambda i, j: (i, jnp.minimum((j + 1) * per, nblk - 1), lb))

    wspec = pl.BlockSpec((1, WIDTH), lambda i, j: (0, 0))
    hmspec = pl.BlockSpec((WIDTH, WIDTH), lambda i, j: (0, 0), pipeline_mode=pl.Buffered(1))
    bspec = pl.BlockSpec((HEADS, QBLK, 2 * QBLK), lambda i, j: (0, 0, 0), pipeline_mode=pl.Buffered(1))
    return pl.pallas_call(
        functools.partial(_attn_kernel, sub=sub),
        grid=(nseq, nt),
        in_specs=[main(0), main(1), prev(1), nxt(1), main(2), prev(2), nxt(2), wspec, wspec, hmspec, bspec],
        out_specs=[pl.BlockSpec((None, tq, WIDTH), lambda i, j: (i, j, 0)),
                   pl.BlockSpec((None, tq, LANES), lambda i, j: (i, j, 0))],
        out_shape=[jax.ShapeDtypeStruct((nseq, sub, WIDTH), BF16),
                   jax.ShapeDtypeStruct((nseq, sub, LANES), F32)],
        compiler_params=_cparams(("parallel", "parallel")),
        name=f"attn_g{group}",
    )(qkv, qkv, qkv, qkv, qkv, qkv, qkv, qw, kw, hm, _alibi_bias(dil, group))


def _merge_kernel(of_ref, ob_ref, z_ref, o0_ref, o1_ref, o2_ref, l0_ref, l1_ref, l2_ref,
                  gate_ref, x_ref, wdn_ref, wat_ref, wo_ref, dnw_ref, n2w_ref, hm_ref, pinv_ref, expand_ref,
                  x1_ref, h2_ref):
    tm = x_ref.shape[0]
    oa = of_ref[...].astype(F32) + ob_ref[...].astype(F32)
    z = z_ref[...].astype(F32)
    gated = oa * lax.rsqrt(_head_meansq(oa, hm_ref[...]) + EPS) * dnw_ref[...] * _silu(z)
    ya = _dot(gated.astype(BF16), wdn_ref[...])
    outs = [o0_ref[...].astype(F32)]
    lses = [l0_ref[...]]
    for idx, (d, oref, lref) in enumerate(((4, o1_ref, l1_ref), (16, o2_ref, l2_ref))):
        pinv = pinv_ref[idx]
        outs.append(_dot(pinv, jnp.concatenate([oref[r] for r in range(d)], axis=0)))
        pieces = jnp.concatenate(_split3(jnp.concatenate([lref[r] for r in range(d)], axis=0)), axis=1)
        lp = _dot(pinv, pieces)
        lses.append(lp[:, 0:LANES] + lp[:, LANES:2 * LANES] + lp[:, 2 * LANES:3 * LANES])
    m = jnp.maximum(jnp.maximum(lses[0], lses[1]), lses[2])
    es = [jnp.exp(l - m) for l in lses]
    den = es[0] + es[1] + es[2]
    expand = expand_ref[...]
    ob = jnp.zeros((tm, WIDTH), F32)
    for e, o in zip(es, outs):
        hi, mid, _ = _split3(e / den)
        ob = ob + _dot(jnp.concatenate([hi, mid], axis=1), expand) * o
    yb = _dot(ob.astype(BF16), wat_ref[...])
    g = gate_ref[...].astype(F32)
    dm = x_ref.shape[1]
    mixed = _sigmoid(g[:, 0:dm]) * ya + _sigmoid(g[:, dm:2 * dm]) * yb
    x1 = x_ref[...] + _dot(mixed.astype(BF16), wo_ref[...])
    x1_ref[...] = x1
    ms = jnp.mean(x1 * x1, axis=-1, keepdims=True)
    h2_ref[...] = (x1 * lax.rsqrt(ms + EPS) * n2w_ref[...]).astype(BF16)


def _merge(o_dn, pa, attn_o, attn_l, gates, x, wdn, wat, wo, dnw, n2w, hm, tm=256):
    b, s, dm = x.shape
    tok = lambda i, j: (i, j, 0)
    const = lambda i, j: (0, 0)
    pinv = jnp.stack([_perm_matrix(tm, 4, inverse=True), _perm_matrix(tm, 16, inverse=True)])
    in_specs = [
        pl.BlockSpec((None, None, tm, WIDTH), lambda i, j: (0, i, j, 0)),
        pl.BlockSpec((None, None, tm, WIDTH), lambda i, j: (1, i, j, 0)),
        pl.BlockSpec((None, tm, WIDTH), lambda i, j: (i, j, 3)),
        pl.BlockSpec((None, tm, WIDTH), tok),
        pl.BlockSpec((None, 4, tm // 4, WIDTH), lambda i, j: (i, 0, j, 0)),
        pl.BlockSpec((None, 16, tm // 16, WIDTH), lambda i, j: (i, 0, j, 0)),
        pl.BlockSpec((None, tm, LANES), tok),
        pl.BlockSpec((None, 4, tm // 4, LANES), lambda i, j: (i, 0, j, 0)),
        pl.BlockSpec((None, 16, tm // 16, LANES), lambda i, j: (i, 0, j, 0)),
        pl.BlockSpec((None, tm, 2 * dm), tok),
        pl.BlockSpec((None, tm, dm), tok),
        pl.BlockSpec((WIDTH, dm), const),
        pl.BlockSpec((WIDTH, dm), const),
        pl.BlockSpec((dm, dm), const),
        pl.BlockSpec((1, WIDTH), const),
        pl.BlockSpec((1, dm), const),
        pl.BlockSpec((WIDTH, WIDTH), const, pipeline_mode=pl.Buffered(1)),
        pl.BlockSpec((2, tm, tm), lambda i, j: (0, 0, 0), pipeline_mode=pl.Buffered(1)),
        pl.BlockSpec((2 * LANES, WIDTH), const, pipeline_mode=pl.Buffered(1)),
    ]
    expand = _head_expand()
    return pl.pallas_call(
        _merge_kernel,
        grid=(b, s // tm),
        in_specs=in_specs,
        out_specs=[pl.BlockSpec((None, tm, dm), tok), pl.BlockSpec((None, tm, dm), tok)],
        out_shape=[jax.ShapeDtypeStruct((b, s, dm), F32), jax.ShapeDtypeStruct((b, s, dm), BF16)],
        compiler_params=_cparams(("parallel", "parallel")),
        name="merge",
    )(o_dn, o_dn, pa, attn_o[0], attn_o[1], attn_o[2], attn_l[0], attn_l[1], attn_l[2],
      gates, x, wdn, wat, wo, dnw, n2w, hm, pinv, jnp.concatenate([expand, expand], axis=0))


def _ffn_kernel(h_ref, hp_ref, hn_ref, wg_ref, wu_ref, cg_ref, cu_ref, wd_ref, x1_ref, o_ref,
                ug_ref, uu_ref, *, tiles_per_seq):
    i = pl.program_id(0)
    j = pl.program_id(1)
    tm = h_ref.shape[0]
    pos = _mod(i, tiles_per_seq)
    hp = hp_ref[...] * jnp.where(pos > 0, 1.0, 0.0).astype(BF16)
    hn = hn_ref[...] * jnp.where(pos < tiles_per_seq - 1, 1.0, 0.0).astype(BF16)
    lhs = jnp.concatenate([hp, h_ref[...], hn], axis=0)

    def conv_branch(w_ref, c_ref, ue_ref):
        ue_ref[...] = _dot(lhs, w_ref[...])
        cw = c_ref[...]
        return (cw[0:1] * ue_ref[BF16_ROWS - 1:BF16_ROWS - 1 + tm, :]
                + cw[1:2] * ue_ref[BF16_ROWS:BF16_ROWS + tm, :]
                + cw[2:3] * ue_ref[BF16_ROWS + 1:BF16_ROWS + 1 + tm, :])

    act = _silu(conv_branch(wg_ref, cg_ref, ug_ref)) * conv_branch(wu_ref, cu_ref, uu_ref)
    part = _dot(act.astype(BF16), wd_ref[...])

    @pl.when(j == 0)
    def _():
        o_ref[...] = x1_ref[...] + part

    @pl.when(j > 0)
    def _():
        o_ref[...] += part


def _ffn(h2, x1, w_up, conv_w, w_down, seq, tm=512, tf=1408):
    n, dm = h2.shape
    dff = w_down.shape[0]
    nf = dff // tf
    per = tm // BF16_ROWS
    nblk = n // BF16_ROWS
    return pl.pallas_call(
        functools.partial(_ffn_kernel, tiles_per_seq=seq // tm),
        grid=(n // tm, nf),
        in_specs=[
            pl.BlockSpec((tm, dm), lambda i, j: (i, 0)),
            pl.BlockSpec((BF16_ROWS, dm), lambda i, j: (jnp.maximum(i * per - 1, 0), 0)),
            pl.BlockSpec((BF16_ROWS, dm), lambda i, j: (jnp.minimum((i + 1) * per, nblk - 1), 0)),
            pl.BlockSpec((dm, tf), lambda i, j: (0, j)),
            pl.BlockSpec((dm, tf), lambda i, j: (0, nf + j)),
            pl.BlockSpec((3, tf), lambda i, j: (0, j)),
            pl.BlockSpec((3, tf), lambda i, j: (0, nf + j)),
            pl.BlockSpec((tf, dm), lambda i, j: (j, 0)),
            pl.BlockSpec((tm, dm), lambda i, j: (i, 0)),
        ],
        out_specs=pl.BlockSpec((tm, dm), lambda i, j: (i, 0)),
        out_shape=jax.ShapeDtypeStruct((n, dm), F32),
        scratch_shapes=[pltpu.VMEM((tm + 2 * BF16_ROWS, tf), F32),
                        pltpu.VMEM((tm + 2 * BF16_ROWS, tf), F32)],
        compiler_params=_cparams(("parallel", "arbitrary")),
        name="ffn",
    )(h2, h2, h2, w_up, w_up, conv_w, conv_w, w_down, x1)


def _layer(x, norm1_w, w_in, dn_conv_w, dn_a_log, dn_dt_bias, dn_out_norm_w, attn_q_norm_w,
           attn_k_norm_w, w_dn_out, w_attn_out, w_o, norm2_w, w_ffn_up, ffn_conv_w, w_ffn_down):
    b, s, dm = x.shape
    n = b * s
    w3 = 3 * WIDTH
    c_z = w3
    c_ab = c_z + WIDTH
    c_q = c_ab + 32
    c_k = c_q + w3
    c_v = c_k + w3
    c_gate = c_v + w3
    wb = w_in.astype(BF16)
    w_a = wb[:, 0:c_ab]
    w_ab = jnp.pad(wb[:, c_ab:c_q], ((0, 0), (0, LANES - 32)))
    w_grp = [jnp.concatenate([wb[:, c0 + g * WIDTH:c0 + (g + 1) * WIDTH] for c0 in (c_q, c_k, c_v)], axis=1)
             for g in range(3)]
    w_gate = wb[:, c_gate:c_gate + 2 * dm]

    hm = _head_mean_matrix()

    pa, gates, ab, qkv0, qkv1, qkv2 = _in_proj(x, norm1_w.reshape(1, dm), w_a, w_gate, w_ab, w_grp)

    q, k, v, cp = _dn_prep(pa, dn_conv_w, ab, dn_a_log.reshape(1, 16), dn_dt_bias.reshape(1, 16), hm)
    o_dn = _dn_scan(q, k, v, cp, _dn_expand_matrix())

    qw = jnp.tile(attn_q_norm_w, HEADS).reshape(1, WIDTH)
    kw = jnp.tile(attn_k_norm_w, HEADS).reshape(1, WIDTH)
    attn_o, attn_l = [], []
    for g, (d, qkv) in enumerate(zip(DILATIONS, (qkv0, qkv1, qkv2))):
        o, l = _attn(qkv.reshape(b * d, s // d, w3), qw, kw, hm, dil=d, group=g)
        shape = (b, s) if d == 1 else (b, d, s // d)
        attn_o.append(o.reshape(shape + (WIDTH,)))
        attn_l.append(l.reshape(shape + (LANES,)))

    x1, h2 = _merge(o_dn, pa, attn_o, attn_l, gates, x, w_dn_out.astype(BF16), w_attn_out.astype(BF16),
                    w_o.astype(BF16), jnp.tile(dn_out_norm_w, HEADS).reshape(1, WIDTH),
                    norm2_w.reshape(1, dm), hm)
    out = _ffn(h2.reshape(n, dm), x1.reshape(n, dm), w_ffn_up.astype(BF16), ffn_conv_w,
               w_ffn_down.astype(BF16), s)
    return out.reshape(b, s, dm)


def kernel(x, norm1_w, w_in, dn_conv_w, dn_a_log, dn_dt_bias, dn_out_norm_w, attn_q_norm_w, attn_k_norm_w, w_dn_out, w_attn_out, w_o, norm2_w, w_ffn_up, ffn_conv_w, w_ffn_down):
    for layer in range(norm1_w.shape[0]):
        x = _layer(x, norm1_w[layer], w_in[layer], dn_conv_w[layer], dn_a_log[layer], dn_dt_bias[layer],
                   dn_out_norm_w[layer], attn_q_norm_w[layer], attn_k_norm_w[layer], w_dn_out[layer],
                   w_attn_out[layer], w_o[layer], norm2_w[layer], w_ffn_up[layer], ffn_conv_w[layer],
                   w_ffn_down[layer])
    return x
```

```python
import functools

import jax
import jax.numpy as jnp
from jax import lax
from jax.experimental import pallas as pl
from jax.experimental.pallas import tpu as pltpu

F32 = jnp.float32
BF16 = jnp.bfloat16

EPS = 1e-6
NEG_INF = -1e30
LOG2E = 1.4426950408889634
LN2 = 0.6931471805599453

HEAD_DIM = 64
HEADS = 8
WIDTH = HEADS * HEAD_DIM
CHUNK = 64
RADIUS = 64
DILATIONS = (1, 4, 16)
ALIBI_MAX = 8.0
N_ATT_HEADS = 24
QUAD = 4 * HEAD_DIM
PHASE_A_PROBLEMS = 16
BF16_ROWS = 16
PERM_TILE = 256
LANES = 128
V7X_VMEM_LIMIT = 56 * 1024 * 1024


def _cparams(sem):
    return pltpu.CompilerParams(dimension_semantics=sem, vmem_limit_bytes=V7X_VMEM_LIMIT)


def _dot(a, b):
    return jnp.dot(a, b, preferred_element_type=F32)


def _dot_nt(a, b):
    return lax.dot_general(a, b, (((1,), (1,)), ((), ())), preferred_element_type=F32)


def _split3(x):
    hi = x.astype(BF16)
    r1 = x - hi.astype(F32)
    mid = r1.astype(BF16)
    lo = (r1 - mid.astype(F32)).astype(BF16)
    return hi, mid, lo


def _dot3(a, x):
    hi, mid, lo = _split3(x)
    return _dot(a, hi) + _dot(a, mid) + _dot(a, lo)


def _log2(n):
    assert n > 0 and n & (n - 1) == 0, n
    return n.bit_length() - 1


def _div(x, n):
    return lax.shift_right_logical(x, _log2(n))


def _mod(x, n):
    assert n & (n - 1) == 0, n
    return x & (n - 1)


def _iota(shape, dim):
    return lax.broadcasted_iota(jnp.int32, shape, dim)


def _sigmoid(x):
    return 0.5 * jnp.tanh(0.5 * x) + 0.5


def _silu(x):
    return x * _sigmoid(x)


def _head_mean_matrix():
    r = jnp.arange(WIDTH)[:, None] // HEAD_DIM
    c = jnp.arange(WIDTH)[None, :] // HEAD_DIM
    return jnp.where(r == c, 1.0 / HEAD_DIM, 0.0).astype(BF16)


def _head_meansq(x, hm):
    return _dot((x * x).astype(BF16), hm)


def _lse_lane(h):
    return h if h % 2 == 0 else HEAD_DIM + h


def _head_expand():
    r = _iota((LANES, WIDTH), 0)
    head = _div(_iota((LANES, WIDTH), 1), HEAD_DIM)
    lane = jnp.where(_mod(head, 2) == 0, head, head + HEAD_DIM)
    return (r == lane).astype(BF16)


def _perm_matrix(tm, d, inverse=False):
    run = tm // d
    row = _iota((tm, tm), 0)
    col = _iota((tm, tm), 1)
    if inverse:
        src = _mod(row, d) * run + _div(row, d)
    else:
        src = _mod(row, run) * d + _div(row, run)
    return (col == src).astype(BF16)


def _in_proj_kernel(x_ref, nw_ref, wa_ref, wgate_ref, wab_ref, wg0_ref, wg1_ref, wg2_ref,
                    pa_ref, gate_ref, ab_ref, q0_ref, q1_ref, q2_ref):
    x = x_ref[...]
    ms = jnp.mean(x * x, axis=-1, keepdims=True)
    h = (x * lax.rsqrt(ms + EPS) * nw_ref[...]).astype(BF16)
    tm = x.shape[0]
    h4 = _dot(_perm_matrix(tm, 4), h).astype(BF16)
    h16 = _dot(_perm_matrix(tm, 16), h).astype(BF16)
    pa_ref[...] = _dot(h, wa_ref[...]).astype(BF16)
    gate_ref[...] = _dot(h, wgate_ref[...]).astype(BF16)
    q0_ref[...] = _dot(h, wg0_ref[...]).astype(BF16)
    ab_ref[...] = _dot(h, wab_ref[...])
    for d, hp, w_ref, ref in ((4, h4, wg1_ref, q1_ref), (16, h16, wg2_ref, q2_ref)):
        run = tm // d
        y = _dot(hp, w_ref[...]).astype(BF16)
        for r in range(d):
            ref[r] = y[r * run:(r + 1) * run]


def _in_proj(x, norm_w, w_a, w_gate, w_ab, w_grp, tm=PERM_TILE):
    b, s, dm = x.shape
    w3 = 3 * WIDTH
    tok = lambda i, j: (i, j, 0)

    def resident(shape):
        return pl.BlockSpec(shape, lambda i, j: (0, 0), pipeline_mode=pl.Buffered(1))

    return pl.pallas_call(
        _in_proj_kernel,
        grid=(b, s // tm),
        in_specs=[pl.BlockSpec((None, tm, dm), tok), resident((1, dm)), resident(w_a.shape),
                  resident(w_gate.shape), resident(w_ab.shape)] + [resident((dm, w3))] * 3,
        out_specs=[
            pl.BlockSpec((None, tm, w_a.shape[1]), tok),
            pl.BlockSpec((None, tm, 2 * dm), tok),
            pl.BlockSpec((None, tm, LANES), tok),
            pl.BlockSpec((None, tm, w3), tok),
            pl.BlockSpec((None, 4, tm // 4, w3), lambda i, j: (i, 0, j, 0)),
            pl.BlockSpec((None, 16, tm // 16, w3), lambda i, j: (i, 0, j, 0)),
        ],
        out_shape=[
            jax.ShapeDtypeStruct((b, s, w_a.shape[1]), BF16),
            jax.ShapeDtypeStruct((b, s, 2 * dm), BF16),
            jax.ShapeDtypeStruct((b, s, LANES), F32),
            jax.ShapeDtypeStruct((b, s, w3), BF16),
            jax.ShapeDtypeStruct((b, 4, s // 4, w3), BF16),
            jax.ShapeDtypeStruct((b, 16, s // 16, w3), BF16),
        ],
        compiler_params=_cparams(("parallel", "parallel")),
        name="in_proj",
    )(x, norm_w, w_a, w_gate, w_ab, *w_grp)


def _shifted(x, prev_row, next_row):
    t = x.shape[0]
    row = _iota((t, 1), 0)
    up = jnp.where(row == 0, prev_row, pltpu.roll(x, 1, 0))
    dn = jnp.where(row == t - 1, next_row, pltpu.roll(x, t - 1, 0))
    return up, dn


def _dn_prep_kernel(x_ref, xp_ref, xn_ref, cw_ref, ab_ref, alog_ref, dtb_ref, hm_ref, tri_ref, place_ref,
                    q_ref, k_ref, v_ref, cp_ref):
    j = pl.program_id(1)
    nj = pl.num_programs(1)
    x = x_ref[...].astype(F32)
    t = x.shape[0]
    prev_row = xp_ref[BF16_ROWS - 1:BF16_ROWS, :].astype(F32) * jnp.where(j > 0, 1.0, 0.0)
    next_row = xn_ref[0:1, :].astype(F32) * jnp.where(j < nj - 1, 1.0, 0.0)
    up, dn = _shifted(x, prev_row, next_row)
    cw = cw_ref[...]
    y = _silu(cw[0:1] * up + cw[1:2] * x + cw[2:3] * dn)
    q = y[:, 0:WIDTH]
    k = y[:, WIDTH:2 * WIDTH]
    hm = hm_ref[...]
    q_ref[...] = (q * lax.rsqrt(_head_meansq(q, hm) * HEAD_DIM + EPS)).astype(BF16)
    k_ref[...] = (k * lax.rsqrt(_head_meansq(k, hm) * HEAD_DIM + EPS)).astype(BF16)
    v_ref[...] = y[:, 2 * WIDTH:3 * WIDTH].astype(BF16)

    ab = ab_ref[...]
    a = ab[:, 0:16] + dtb_ref[...]
    softplus = jnp.maximum(a, 0.0) + jnp.log(1.0 + jnp.exp(-jnp.abs(a)))
    g = -jnp.exp(alog_ref[...]) * softplus
    beta = _sigmoid(ab[:, 16:32])
    lane = _iota((t, 16), 1)
    gc = jnp.where(lane < HEADS, _dot3(tri_ref[0], g), _dot3(tri_ref[1], g))
    acc = jnp.zeros((t, LANES), F32)
    for idx, piece in enumerate(_split3(gc) + _split3(beta)):
        acc = acc + _dot(piece, place_ref[idx])
    cp_ref[...] = acc.astype(BF16)


def _dn_prep_constants(tp):
    r = jnp.arange(tp)[:, None]
    c = jnp.arange(tp)[None, :]
    same = (r // CHUNK) == (c // CHUNK)
    tri = jnp.stack([same & (r >= c), same & (r <= c)]).astype(BF16)
    idx = jnp.arange(6)[:, None, None]
    place = (jnp.arange(LANES)[None, None, :] == jnp.arange(16)[None, :, None] + 16 * idx).astype(BF16)
    return tri, place


def _dn_prep(pa, conv_w, ab, alog, dtb, hm, tp=256):
    b, s, _ = pa.shape
    w3 = 3 * WIDTH
    nblk = s // BF16_ROWS
    per = tp // BF16_ROWS
    tri, place = _dn_prep_constants(tp)
    const2 = lambda i, j: (0, 0)
    const3 = lambda i, j: (0, 0, 0)
    return pl.pallas_call(
        _dn_prep_kernel,
        grid=(b, s // tp),
        in_specs=[
            pl.BlockSpec((None, tp, w3), lambda i, j: (i, j, 0)),
            pl.BlockSpec((None, BF16_ROWS, w3), lambda i, j: (i, jnp.maximum(j * per - 1, 0), 0)),
            pl.BlockSpec((None, BF16_ROWS, w3), lambda i, j: (i, jnp.minimum((j + 1) * per, nblk - 1), 0)),
            pl.BlockSpec((3, w3), lambda i, j: (0, 0)),
            pl.BlockSpec((None, tp, LANES), lambda i, j: (i, j, 0)),
            pl.BlockSpec((1, 16), const2),
            pl.BlockSpec((1, 16), const2),
            pl.BlockSpec((WIDTH, WIDTH), const2, pipeline_mode=pl.Buffered(1)),
            pl.BlockSpec((2, tp, tp), const3, pipeline_mode=pl.Buffered(1)),
            pl.BlockSpec((6, 16, LANES), const3, pipeline_mode=pl.Buffered(1)),
        ],
        out_specs=[pl.BlockSpec((None, tp, WIDTH), lambda i, j: (i, j, 0))] * 3
        + [pl.BlockSpec((None, tp, LANES), lambda i, j: (i, j, 0))],
        out_shape=[jax.ShapeDtypeStruct((b, s, WIDTH), BF16)] * 3
        + [jax.ShapeDtypeStruct((b, s, LANES), BF16)],
        compiler_params=_cparams(("parallel", "parallel")),
        name="dn_prep",
    )(pa, pa, pa, conv_w, ab, alog, dtb, hm, tri, place)


def _block_diag(x):
    blk = _div(_iota(x.shape, 1), HEAD_DIM)
    parts = [jnp.where(blk == a, x, 0.0) for a in range(4)]
    return jnp.concatenate(parts, axis=0).astype(BF16)


def _dn_scan_kernel(q_ref, k_ref, v_ref, cp_ref, e2_ref, o_ref,
                    s_scr, u_scr, l1_scr, l2_scr, al_scr):
    dirn = pl.program_id(1)
    seg = pl.program_id(2)
    nb = q_ref.shape[0]
    nchunk = q_ref.shape[1] // CHUNK
    fwd = dirn == 0

    @pl.when(seg == 0)
    def _():
        s_scr[...] = jnp.zeros_like(s_scr)

    row = _iota((CHUNK, QUAD), 0)
    col = _mod(_iota((CHUNK, QUAD), 1), HEAD_DIM)
    later = jnp.where(fwd, row, col)
    earlier = jnp.where(fwd, col, row)
    incl = later >= earlier
    strict = later > earlier
    diag = row == col
    eye = diag.astype(BF16)
    unroll = max(1, PHASE_A_PROBLEMS // (2 * nb))

    def phase_a(it, carry):
        probs = []
        for u in range(unroll):
            c = it * unroll + u
            rows = pl.ds(pl.multiple_of(c * CHUNK, CHUNK), CHUNK)
            x_all = _dot(jnp.concatenate([cp_ref[bi, rows, :] for bi in range(nb)], axis=0), e2_ref[...])
            for bi in range(nb):
                x = x_all[bi * CHUNK:(bi + 1) * CHUNK]
                for qd in range(2):
                    lanes = slice(qd * QUAD, (qd + 1) * QUAD)
                    probs.append(dict(
                        c=c, slot=2 * bi + qd, gcol=x[:, qd * QUAD:(qd + 1) * QUAD],
                        beta=x[:, WIDTH + qd * QUAD:WIDTH + (qd + 1) * QUAD],
                        kq=k_ref[bi, rows, lanes], qq=q_ref[bi, rows, lanes], vq=v_ref[bi, rows, lanes]))
        for pr in probs:
            pr["kf"] = pr["kq"].astype(F32)
            pr["z"] = _dot_nt(jnp.concatenate([pr["kq"], pr["qq"], eye], axis=0),
                              _block_diag(pr["kf"]))
        for pr in probs:
            gcol, beta, z = pr["gcol"], pr["beta"], pr["z"]
            grow = jnp.sum(jnp.where(diag, gcol, 0.0), axis=0, keepdims=True)
            glast = jnp.where(fwd, gcol[CHUNK - 1:CHUNK, :], gcol[0:1, :])
            dm = jnp.exp(jnp.where(incl, gcol - grow, NEG_INF))
            eg = jnp.exp(gcol)
            a = jnp.where(strict, beta * z[0:CHUNK] * dm, 0.0)
            qkm = jnp.where(incl, z[CHUNK:2 * CHUNK] * dm, 0.0) * 0.125
            kdt = z[2 * CHUNK:3 * CHUNK] * jnp.exp(glast - grow)
            pr["vb"] = pr["vq"].astype(F32) * beta
            pr["kbg"] = pr["kf"] * (beta * eg)
            qdec = pr["qq"].astype(F32) * (eg * 0.125)
            c, slot = pr["c"], pr["slot"]
            l2_scr[slot, c] = jnp.concatenate([qkm, kdt], axis=0).astype(BF16)
            l1_scr[slot, c, CHUNK:2 * CHUNK, :] = qdec.astype(BF16)
            al_scr[slot, c] = jnp.broadcast_to(jnp.exp(glast), (8, QUAD))
            pr["a"] = a
        for pr in probs:
            pr["p"] = _dot(pr["a"].astype(BF16), _block_diag(pr["a"]))
            pr["n"] = -pr["a"]
        for _ in range(4):
            for pr in probs:
                n, p = pr["n"], pr["p"]
                zz = _dot(jnp.concatenate([n, p], axis=0).astype(BF16), _block_diag(p))
                pr["n"] = n + p + zz[0:CHUNK]
                pr["p"] = zz[CHUNK:2 * CHUNK]
        for pr in probs:
            n, p = pr["n"], pr["p"]
            pr["n"] = n + p + _dot(n.astype(BF16), _block_diag(p))
        for pr in probs:
            vb, kbg, c, slot = pr["vb"], pr["kbg"], pr["c"], pr["slot"]
            rhs = jnp.concatenate([_block_diag(vb), _block_diag(kbg)], axis=1)
            uw = _dot(pr["n"].astype(BF16), rhs)
            u_scr[slot, c] = vb + uw[:, 0:QUAD]
            l1_scr[slot, c, 0:CHUNK, :] = (kbg + uw[:, QUAD:2 * QUAD]).astype(BF16)
        return carry

    lax.fori_loop(0, nchunk // unroll, phase_a, 0)

    nslot = 2 * nb

    def phase_b(i, states):
        c = jnp.where(fwd, i, nchunk - 1 - i)
        r0 = pl.multiple_of(c * CHUNK, CHUNK)
        z1 = [_dot(l1_scr[sl, c], _block_diag(states[sl])) for sl in range(nslot)]
        vn = [u_scr[sl, c] - z1[sl][0:CHUNK] for sl in range(nslot)]
        z2 = [_dot(l2_scr[sl, c], _block_diag(vn[sl])) for sl in range(nslot)]
        new = []
        for sl in range(nslot):
            o = z1[sl][CHUNK:2 * CHUNK] + z2[sl][0:CHUNK]
            o_ref[sl // 2, pl.ds(r0, CHUNK), (sl % 2) * QUAD:(sl % 2 + 1) * QUAD] = o.astype(o_ref.dtype)
            new.append(al_scr[sl, c][0:1, :] * states[sl] + z2[sl][CHUNK:2 * CHUNK])
        return tuple(new)

    final = lax.fori_loop(0, nchunk, phase_b, tuple(s_scr[sl] for sl in range(nslot)))
    for sl in range(nslot):
        s_scr[sl] = final[sl]


def _dn_scan(q, k, v, cp, e2, nb=4, seg=512):
    b, s, _ = q.shape
    while b % nb:
        nb //= 2
    seg = min(seg, s)
    nseg = s // seg
    nchunk = seg // CHUNK

    def tok(i, d, j):
        return (i, jnp.where(d == 0, j, nseg - 1 - j), 0)

    return pl.pallas_call(
        _dn_scan_kernel,
        grid=(b // nb, 2, nseg),
        in_specs=[pl.BlockSpec((nb, seg, WIDTH), tok)] * 3
        + [pl.BlockSpec((nb, seg, LANES), tok),
           pl.BlockSpec((None, LANES, 2 * WIDTH), lambda i, d, j: (d, 0, 0))],
        out_specs=pl.BlockSpec((None, nb, seg, WIDTH),
                               lambda i, d, j: (d, i, jnp.where(d == 0, j, nseg - 1 - j), 0)),
        out_shape=jax.ShapeDtypeStruct((2, b, s, WIDTH), BF16),
        scratch_shapes=[
            pltpu.VMEM((2 * nb, CHUNK, QUAD), F32),
            pltpu.VMEM((2 * nb, nchunk, CHUNK, QUAD), F32),
            pltpu.VMEM((2 * nb, nchunk, 2 * CHUNK, QUAD), BF16),
            pltpu.VMEM((2 * nb, nchunk, 2 * CHUNK, QUAD), BF16),
            pltpu.VMEM((2 * nb, nchunk, 8, QUAD), F32),
        ],
        compiler_params=_cparams(("parallel", "arbitrary", "arbitrary")),
        name="dn_scan",
    )(q, k, v, cp, e2)


def _dn_expand_matrix():
    r = jnp.arange(LANES)[:, None]
    c = jnp.arange(2 * WIDTH)[None, :]
    piece, lane = r // 16, r % 16
    out = []
    for d in range(2):
        head = lane - 8 * d
        ok = (piece < 6) & (head >= 0) & (head < HEADS)
        ok = ok & ((c // WIDTH) == (piece // 3)) & (((c % WIDTH) // HEAD_DIM) == head)
        out.append(ok)
    return jnp.stack(out).astype(BF16)


QBLK = 2 * RADIUS
ATTN_ROWS = 1024


def _attn_kernel(q_ref, k_ref, kp_ref, kn_ref, v_ref, vp_ref, vn_ref, qw_ref, kw_ref, hm_ref, bias_ref,
                 o_ref, lse_ref, *, sub):
    for si in range(q_ref.shape[0]):
        _attn_tile(*(r.at[si] for r in (q_ref, k_ref, kp_ref, kn_ref, v_ref, vp_ref, vn_ref)),
                   qw_ref, kw_ref, hm_ref, bias_ref, o_ref.at[si], lse_ref.at[si], sub=sub)


def _attn_tile(q_ref, k_ref, kp_ref, kn_ref, v_ref, vp_ref, vn_ref, qw_ref, kw_ref, hm_ref, bias_ref,
               o_ref, lse_ref, *, sub):
    t = pl.program_id(1)
    tq = q_ref.shape[0]
    nqb = tq // QBLK
    low = _mod(_iota((1, WIDTH), 1), LANES) < HEAD_DIM
    hm = hm_ref[...]
    q = q_ref[...].astype(F32)
    qn = q * lax.rsqrt(_dot((q * q).astype(BF16), hm) + EPS) * (qw_ref[...] * (HEAD_DIM ** -0.5 * LOG2E))
    qn = qn.astype(BF16)
    zero = jnp.zeros((), BF16)
    one = jnp.ones((), BF16)
    qsel = (jnp.where(low, qn, zero), jnp.where(low, zero, qn))
    kx = jnp.concatenate([kp_ref[...], k_ref[...], kn_ref[...]], axis=0).astype(F32)
    kn = (kx * lax.rsqrt(_dot((kx * kx).astype(BF16), hm) + EPS) * kw_ref[...]).astype(BF16)
    vx = jnp.concatenate([vp_ref[...], v_ref[...], vn_ref[...]], axis=0)
    vsel = (jnp.where(low, vx, one), jnp.where(low, one, vx))
    c = _iota((QBLK, 2 * QBLK), 1)
    lane = _iota((1, LANES), 1)
    low_pair = lane < HEAD_DIM
    pairs = [slice((h // 2) * LANES, (h // 2 + 1) * LANES) for h in range(HEADS)]
    for qb in range(nqb):
        rows = slice(qb * QBLK, (qb + 1) * QBLK)
        krows = slice(qb * QBLK, qb * QBLK + 2 * QBLK)
        scores = [_dot_nt(qsel[h % 2][rows, pairs[h]], kn[krows, pairs[h]]) for h in range(HEADS)]
        edge = None
        if qb == 0 or qb == nqb - 1:
            kj = t * tq + (qb * QBLK - RADIUS) + c
            edge = jnp.where((kj >= 0) & (kj < sub), 0.0, NEG_INF)
        probs, maxes = [], []
        for h in range(HEADS):
            sc = scores[h] + bias_ref[h]
            if edge is not None:
                sc = sc + edge
            m = jnp.max(sc, axis=-1, keepdims=True)
            probs.append(jnp.exp2(sc - m).astype(BF16))
            maxes.append(m)
        outs = [_dot(probs[h], vsel[h % 2][krows, pairs[h]]) for h in range(HEADS)]
        lse_tile = jnp.zeros((QBLK, LANES), F32)
        for h0 in range(0, HEADS, 2):
            even, odd = outs[h0], outs[h0 + 1]
            num = jnp.where(low_pair, even, odd)
            den = pltpu.roll(jnp.where(low_pair, odd, even), HEAD_DIM, 1)
            o_ref[rows, pairs[h0]] = (num / den).astype(o_ref.dtype)
            lse_pair = jnp.where(low_pair, maxes[h0], maxes[h0 + 1]) * LN2 + jnp.log(den)
            keep = (lane == _lse_lane(h0)) | (lane == _lse_lane(h0 + 1))
            lse_tile = lse_tile + jnp.where(keep, lse_pair, 0.0)
        lse_ref[rows, :] = lse_tile


def _alibi_bias(dil, group):
    a = jnp.arange(QBLK)[:, None]
    c = jnp.arange(2 * QBLK)[None, :]
    rel = jnp.abs(a + RADIUS - c)
    slopes = 2.0 ** (-ALIBI_MAX * (group * HEADS + jnp.arange(1, HEADS + 1, dtype=F32)) / N_ATT_HEADS)
    bias = -(slopes * LOG2E)[:, None, None] * (rel * dil).astype(F32)[None]
    return jnp.where((rel <= RADIUS)[None], bias, NEG_INF)


def _attn(qkv, qw, kw, hm, *, dil, group):
    nseq, sub, _ = qkv.shape
    tq = min(ATTN_ROWS, sub)
    ns = max(1, ATTN_ROWS // sub)
    while nseq % ns:
        ns //= 2
    nt = sub // tq
    per = tq // RADIUS
    nblk = sub // RADIUS

    def main(lb):
        return pl.BlockSpec((ns, tq, WIDTH), lambda i, j: (i, j, lb))

    def prev(lb):
        return pl.BlockSpec((ns, RADIUS, WIDTH), lambda i, j: (i, jnp.maximum(j * per - 1, 0), lb))

    def nxt(lb):
        return pl.BlockSpec((ns, RADIUS, WIDTH),
                            lambda i, j: (i, jnp.minimum((j + 1) * per, nblk - 1), lb))

    wspec = pl.BlockSpec((1, WIDTH), lambda i, j: (0, 0))
    hmspec = pl.BlockSpec((WIDTH, WIDTH), lambda i, j: (0, 0), pipeline_mode=pl.Buffered(1))
    bspec = pl.BlockSpec((HEADS, QBLK, 2 * QBLK), lambda i, j: (0, 0, 0), pipeline_mode=pl.Buffered(1))
    return pl.pallas_call(
        functools.partial(_attn_kernel, sub=sub),
        grid=(nseq // ns, nt),
        in_specs=[main(0), main(1), prev(1), nxt(1), main(2), prev(2), nxt(2), wspec, wspec, hmspec, bspec],
        out_specs=[pl.BlockSpec((ns, tq, WIDTH), lambda i, j: (i, j, 0)),
                   pl.BlockSpec((ns, tq, LANES), lambda i, j: (i, j, 0))],
        out_shape=[jax.ShapeDtypeStruct((nseq, sub, WIDTH), BF16),
                   jax.ShapeDtypeStruct((nseq, sub, LANES), F32)],
        compiler_params=_cparams(("parallel", "parallel")),
        name=f"attn_g{group}",
    )(qkv, qkv, qkv, qkv, qkv, qkv, qkv, qw, kw, hm, _alibi_bias(dil, group))


def _merge_kernel(of_ref, ob_ref, z_ref, o0_ref, o1_ref, o2_ref, l0_ref, l1_ref, l2_ref,
                  gate_ref, x_ref, wdn_ref, wat_ref, wo_ref, dnw_ref, n2w_ref, hm_ref, pinv_ref, expand_ref,
                  x1_ref, h2_ref):
    tm = x_ref.shape[0]
    dm = x_ref.shape[1]
    sub = pinv_ref.shape[1]
    expand = expand_ref[...]
    blocks = [dict(rows=slice(t * sub, (t + 1) * sub), t=t) for t in range(tm // sub)]
    for bk in blocks:
        rows, t = bk["rows"], bk["t"]
        oa = of_ref[rows, :].astype(F32) + ob_ref[rows, :].astype(F32)
        bk["oa"] = oa
        bk["msq"] = _head_meansq(oa, hm_ref[...])
        outs = [o0_ref[rows, :].astype(F32)]
        lses = [l0_ref[rows, :]]
        for idx, (d, oref, lref) in enumerate(((4, o1_ref, l1_ref), (16, o2_ref, l2_ref))):
            pinv = pinv_ref[idx]
            run = sub // d
            part = slice(t * run, (t + 1) * run)
            outs.append(_dot(pinv, jnp.concatenate([oref[r, part, :] for r in range(d)], axis=0)))
            lflat = jnp.concatenate([lref[r, part, :] for r in range(d)], axis=0)
            lp = _dot(pinv, jnp.concatenate(_split3(lflat), axis=1))
            lses.append(lp[:, 0:LANES] + lp[:, LANES:2 * LANES] + lp[:, 2 * LANES:3 * LANES])
        bk["outs"], bk["lses"] = outs, lses
    for bk in blocks:
        rows = bk["rows"]
        z = z_ref[rows, :].astype(F32)
        gated = bk["oa"] * lax.rsqrt(bk["msq"] + EPS) * dnw_ref[...] * _silu(z)
        bk["ya"] = _dot(gated.astype(BF16), wdn_ref[...])
        lses = bk["lses"]
        m = jnp.maximum(jnp.maximum(lses[0], lses[1]), lses[2])
        es = [jnp.exp(l - m) for l in lses]
        den = es[0] + es[1] + es[2]
        ob = jnp.zeros((sub, WIDTH), F32)
        for e, o in zip(es, bk["outs"]):
            hi, mid, _ = _split3(e / den)
            ob = ob + _dot(jnp.concatenate([hi, mid], axis=1), expand) * o
        bk["ob"] = ob
    for bk in blocks:
        bk["yb"] = _dot(bk["ob"].astype(BF16), wat_ref[...])
    for bk in blocks:
        rows = bk["rows"]
        g = gate_ref[rows, :].astype(F32)
        mixed = _sigmoid(g[:, 0:dm]) * bk["ya"] + _sigmoid(g[:, dm:2 * dm]) * bk["yb"]
        x1 = x_ref[rows, :] + _dot(mixed.astype(BF16), wo_ref[...])
        x1_ref[rows, :] = x1
        ms = jnp.mean(x1 * x1, axis=-1, keepdims=True)
        h2_ref[rows, :] = (x1 * lax.rsqrt(ms + EPS) * n2w_ref[...]).astype(BF16)


def _merge(o_dn, pa, attn_o, attn_l, gates, x, wdn, wat, wo, dnw, n2w, hm, tm=2 * PERM_TILE):
    b, s, dm = x.shape
    tok = lambda i, j: (i, j, 0)
    const = lambda i, j: (0, 0)
    pinv = jnp.stack([_perm_matrix(PERM_TILE, 4, inverse=True), _perm_matrix(PERM_TILE, 16, inverse=True)])
    in_specs = [
        pl.BlockSpec((None, None, tm, WIDTH), lambda i, j: (0, i, j, 0)),
        pl.BlockSpec((None, None, tm, WIDTH), lambda i, j: (1, i, j, 0)),
        pl.BlockSpec((None, tm, WIDTH), lambda i, j: (i, j, 3)),
        pl.BlockSpec((None, tm, WIDTH), tok),
        pl.BlockSpec((None, 4, tm // 4, WIDTH), lambda i, j: (i, 0, j, 0)),
        pl.BlockSpec((None, 16, tm // 16, WIDTH), lambda i, j: (i, 0, j, 0)),
        pl.BlockSpec((None, tm, LANES), tok),
        pl.BlockSpec((None, 4, tm // 4, LANES), lambda i, j: (i, 0, j, 0)),
        pl.BlockSpec((None, 16, tm // 16, LANES), lambda i, j: (i, 0, j, 0)),
        pl.BlockSpec((None, tm, 2 * dm), tok),
        pl.BlockSpec((None, tm, dm), tok),
        pl.BlockSpec((WIDTH, dm), const),
        pl.BlockSpec((WIDTH, dm), const),
        pl.BlockSpec((dm, dm), const),
        pl.BlockSpec((1, WIDTH), const),
        pl.BlockSpec((1, dm), const),
        pl.BlockSpec((WIDTH, WIDTH), const, pipeline_mode=pl.Buffered(1)),
        pl.BlockSpec((2, PERM_TILE, PERM_TILE), lambda i, j: (0, 0, 0), pipeline_mode=pl.Buffered(1)),
        pl.BlockSpec((2 * LANES, WIDTH), const, pipeline_mode=pl.Buffered(1)),
    ]
    expand = _head_expand()
    return pl.pallas_call(
        _merge_kernel,
        grid=(b, s // tm),
        in_specs=in_specs,
        out_specs=[pl.BlockSpec((None, tm, dm), tok), pl.BlockSpec((None, tm, dm), tok)],
        out_shape=[jax.ShapeDtypeStruct((b, s, dm), F32), jax.ShapeDtypeStruct((b, s, dm), BF16)],
        compiler_params=_cparams(("parallel", "parallel")),
        name="merge",
    )(o_dn, o_dn, pa, attn_o[0], attn_o[1], attn_o[2], attn_l[0], attn_l[1], attn_l[2],
      gates, x, wdn, wat, wo, dnw, n2w, hm, pinv, jnp.concatenate([expand, expand], axis=0))


def _ffn_kernel(h_ref, hp_ref, hn_ref, wup_ref, cw_ref, wd_ref, x1_ref, o_ref,
                ug_ref, uu_ref, *, tiles_per_seq, chunks):
    i = pl.program_id(0)
    tm = h_ref.shape[0]
    dff = wd_ref.shape[0]
    pos = _mod(i, tiles_per_seq)
    hp = hp_ref[...] * jnp.where(pos > 0, 1.0, 0.0).astype(BF16)
    hn = hn_ref[...] * jnp.where(pos < tiles_per_seq - 1, 1.0, 0.0).astype(BF16)
    lhs = jnp.concatenate([hp, h_ref[...], hn], axis=0)

    def conv(c0, c1, ue_ref):
        w = c1 - c0
        cw = cw_ref[:, c0:c1]
        return (cw[0:1] * ue_ref[BF16_ROWS - 1:BF16_ROWS - 1 + tm, 0:w]
                + cw[1:2] * ue_ref[BF16_ROWS:BF16_ROWS + tm, 0:w]
                + cw[2:3] * ue_ref[BF16_ROWS + 1:BF16_ROWS + 1 + tm, 0:w])

    for idx, (c0, c1) in enumerate(chunks):
        ug_ref[idx, :, 0:c1 - c0] = _dot(lhs, wup_ref[:, c0:c1])
        uu_ref[idx, :, 0:c1 - c0] = _dot(lhs, wup_ref[:, dff + c0:dff + c1])
    out = x1_ref[...]
    for idx, (c0, c1) in enumerate(chunks):
        act = _silu(conv(c0, c1, ug_ref.at[idx])) * conv(dff + c0, dff + c1, uu_ref.at[idx])
        out = out + _dot(act.astype(BF16), wd_ref[c0:c1, :])
    o_ref[...] = out


MXU_TILE = 256
FFN_CHUNK_TILES = 4


def _ffn(h2, x1, w_up, conv_w, w_down, seq, tm=512):
    n, dm = h2.shape
    dff = w_down.shape[0]
    split = FFN_CHUNK_TILES * MXU_TILE
    chunks = tuple((c0, min(c0 + split, dff)) for c0 in range(0, dff, split))
    per = tm // BF16_ROWS
    nblk = n // BF16_ROWS
    const = lambda i: (0, 0)

    def resident(shape):
        return pl.BlockSpec(shape, const, pipeline_mode=pl.Buffered(1))

    return pl.pallas_call(
        functools.partial(_ffn_kernel, tiles_per_seq=seq // tm, chunks=chunks),
        grid=(n // tm,),
        in_specs=[
            pl.BlockSpec((tm, dm), lambda i: (i, 0)),
            pl.BlockSpec((BF16_ROWS, dm), lambda i: (jnp.maximum(i * per - 1, 0), 0)),
            pl.BlockSpec((BF16_ROWS, dm), lambda i: (jnp.minimum((i + 1) * per, nblk - 1), 0)),
            resident(w_up.shape),
            resident(conv_w.shape),
            resident(w_down.shape),
            pl.BlockSpec((tm, dm), lambda i: (i, 0)),
        ],
        out_specs=pl.BlockSpec((tm, dm), lambda i: (i, 0)),
        out_shape=jax.ShapeDtypeStruct((n, dm), F32),
        scratch_shapes=[pltpu.VMEM((len(chunks), tm + 2 * BF16_ROWS, split), F32),
                        pltpu.VMEM((len(chunks), tm + 2 * BF16_ROWS, split), F32)],
        compiler_params=_cparams(("parallel",)),
        name="ffn",
    )(h2, h2, h2, w_up, conv_w, w_down, x1)


def _layer(x, norm1_w, w_in, dn_conv_w, dn_a_log, dn_dt_bias, dn_out_norm_w, attn_q_norm_w,
           attn_k_norm_w, w_dn_out, w_attn_out, w_o, norm2_w, w_ffn_up, ffn_conv_w, w_ffn_down):
    b, s, dm = x.shape
    n = b * s
    w3 = 3 * WIDTH
    c_z = w3
    c_ab = c_z + WIDTH
    c_q = c_ab + 32
    c_k = c_q + w3
    c_v = c_k + w3
    c_gate = c_v + w3
    wb = w_in.astype(BF16)
    w_a = wb[:, 0:c_ab]
    w_ab = jnp.pad(wb[:, c_ab:c_q], ((0, 0), (0, LANES - 32)))
    w_grp = [jnp.concatenate([wb[:, c0 + g * WIDTH:c0 + (g + 1) * WIDTH] for c0 in (c_q, c_k, c_v)], axis=1)
             for g in range(3)]
    w_gate = wb[:, c_gate:c_gate + 2 * dm]

    hm = _head_mean_matrix()

    pa, gates, ab, qkv0, qkv1, qkv2 = _in_proj(x, norm1_w.reshape(1, dm), w_a, w_gate, w_ab, w_grp)

    q, k, v, cp = _dn_prep(pa, dn_conv_w, ab, dn_a_log.reshape(1, 16), dn_dt_bias.reshape(1, 16), hm)
    o_dn = _dn_scan(q, k, v, cp, _dn_expand_matrix())

    qw = jnp.tile(attn_q_norm_w, HEADS).reshape(1, WIDTH)
    kw = jnp.tile(attn_k_norm_w, HEADS).reshape(1, WIDTH)
    attn_o, attn_l = [], []
    for g, (d, qkv) in enumerate(zip(DILATIONS, (qkv0, qkv1, qkv2))):
        o, l = _attn(qkv.reshape(b * d, s // d, w3), qw, kw, hm, dil=d, group=g)
        shape = (b, s) if d == 1 else (b, d, s // d)
        attn_o.append(o.reshape(shape + (WIDTH,)))
        attn_l.append(l.reshape(shape + (LANES,)))

    x1, h2 = _merge(o_dn, pa, attn_o, attn_l, gates, x, w_dn_out.astype(BF16), w_attn_out.astype(BF16),
                    w_o.astype(BF16), jnp.tile(dn_out_norm_w, HEADS).reshape(1, WIDTH),
                    norm2_w.reshape(1, dm), hm)
    out = _ffn(h2.reshape(n, dm), x1.reshape(n, dm), w_ffn_up.astype(BF16), ffn_conv_w,
               w_ffn_down.astype(BF16), s)
    return out.reshape(b, s, dm)


def kernel(x, norm1_w, w_in, dn_conv_w, dn_a_log, dn_dt_bias, dn_out_norm_w, attn_q_norm_w, attn_k_norm_w, w_dn_out, w_attn_out, w_o, norm2_w, w_ffn_up, ffn_conv_w, w_ffn_down):
    for layer in range(norm1_w.shape[0]):
        x = _layer(x, norm1_w[layer], w_in[layer], dn_conv_w[layer], dn_a_log[layer], dn_dt_bias[layer],
                   dn_out_norm_w[layer], attn_q_norm_w[layer], attn_k_norm_w[layer], w_dn_out[layer],
                   w_attn_out[layer], w_o[layer], norm2_w[layer], w_ffn_up[layer], ffn_conv_w[layer],
                   w_ffn_down[layer])
    return x
```

```python
import functools

import jax
import jax.numpy as jnp
from jax import lax
from jax.experimental import pallas as pl
from jax.experimental.pallas import tpu as pltpu

F32 = jnp.float32
BF16 = jnp.bfloat16

EPS = 1e-6
NEG_INF = -1e30
LOG2E = 1.4426950408889634
LN2 = 0.6931471805599453

HEAD_DIM = 64
HEADS = 8
WIDTH = HEADS * HEAD_DIM
CHUNK = 64
RADIUS = 64
DILATIONS = (1, 4, 16)
ALIBI_MAX = 8.0
N_ATT_HEADS = 24
QUAD = 4 * HEAD_DIM
PHASE_A_PROBLEMS = 16
BF16_ROWS = 16
PERM_TILE = 256
LANES = 128
V7X_VMEM_LIMIT = 56 * 1024 * 1024


def _cparams(sem):
    return pltpu.CompilerParams(dimension_semantics=sem, vmem_limit_bytes=V7X_VMEM_LIMIT)


def _dot(a, b):
    return jnp.dot(a, b, preferred_element_type=F32)


def _dot_nt(a, b):
    return lax.dot_general(a, b, (((1,), (1,)), ((), ())), preferred_element_type=F32)


def _split3(x):
    hi = x.astype(BF16)
    r1 = x - hi.astype(F32)
    mid = r1.astype(BF16)
    lo = (r1 - mid.astype(F32)).astype(BF16)
    return hi, mid, lo


def _dot3(a, x):
    hi, mid, lo = _split3(x)
    return _dot(a, hi) + _dot(a, mid) + _dot(a, lo)


def _log2(n):
    assert n > 0 and n & (n - 1) == 0, n
    return n.bit_length() - 1


def _div(x, n):
    return lax.shift_right_logical(x, _log2(n))


def _mod(x, n):
    assert n & (n - 1) == 0, n
    return x & (n - 1)


def _iota(shape, dim):
    return lax.broadcasted_iota(jnp.int32, shape, dim)


def _sigmoid(x):
    return 0.5 * jnp.tanh(0.5 * x) + 0.5


def _silu(x):
    h = 0.5 * x
    return h + h * jnp.tanh(h)


def _head_mean_matrix():
    r = jnp.arange(WIDTH)[:, None] // HEAD_DIM
    c = jnp.arange(WIDTH)[None, :] // HEAD_DIM
    return jnp.where(r == c, 1.0 / HEAD_DIM, 0.0).astype(BF16)


def _head_meansq(x, hm):
    return _dot((x * x).astype(BF16), hm)


def _lse_lane(h):
    return h if h % 2 == 0 else HEAD_DIM + h


def _head_expand():
    r = _iota((LANES, WIDTH), 0)
    head = _div(_iota((LANES, WIDTH), 1), HEAD_DIM)
    lane = jnp.where(_mod(head, 2) == 0, head, head + HEAD_DIM)
    return (r == lane).astype(BF16)


def _perm_matrix(tm, d, inverse=False):
    run = tm // d
    row = _iota((tm, tm), 0)
    col = _iota((tm, tm), 1)
    if inverse:
        src = _mod(row, d) * run + _div(row, d)
    else:
        src = _mod(row, run) * d + _div(row, run)
    return (col == src).astype(BF16)


def _in_proj_kernel(x_ref, nw_ref, wa_ref, wgate_ref, wab_ref, wg0_ref, wg1_ref, wg2_ref,
                    pa_ref, gate_ref, ab_ref, q0_ref, q1_ref, q2_ref):
    x = x_ref[...]
    ms = jnp.mean(x * x, axis=-1, keepdims=True)
    h = (x * lax.rsqrt(ms + EPS) * nw_ref[...]).astype(BF16)
    tm = x.shape[0]
    nblk = tm // PERM_TILE
    perm = {}
    for d in (4, 16):
        pm = _perm_matrix(PERM_TILE, d)
        perm[d] = jnp.concatenate(
            [_dot(pm, h[t * PERM_TILE:(t + 1) * PERM_TILE]).astype(BF16) for t in range(nblk)], axis=0)
    pa_ref[...] = _dot(h, wa_ref[...]).astype(BF16)
    gate_ref[...] = _dot(h, wgate_ref[...]).astype(BF16)
    q0_ref[...] = _dot(h, wg0_ref[...]).astype(BF16)
    ab_ref[...] = _dot(h, wab_ref[...])
    for d, w_ref, ref in ((4, wg1_ref, q1_ref), (16, wg2_ref, q2_ref)):
        run = PERM_TILE // d
        y = _dot(perm[d], w_ref[...]).astype(BF16)
        for t in range(nblk):
            for r in range(d):
                ref[r, t * run:(t + 1) * run, :] = y[t * PERM_TILE + r * run:t * PERM_TILE + (r + 1) * run]


def _in_proj(x, norm_w, w_a, w_gate, w_ab, w_grp, tm=2 * PERM_TILE):
    b, s, dm = x.shape
    w3 = 3 * WIDTH
    tok = lambda i, j: (i, j, 0)

    def resident(shape):
        return pl.BlockSpec(shape, lambda i, j: (0, 0), pipeline_mode=pl.Buffered(1))

    return pl.pallas_call(
        _in_proj_kernel,
        grid=(b, s // tm),
        in_specs=[pl.BlockSpec((None, tm, dm), tok), resident((1, dm)), resident(w_a.shape),
                  resident(w_gate.shape), resident(w_ab.shape)] + [resident((dm, w3))] * 3,
        out_specs=[
            pl.BlockSpec((None, tm, w_a.shape[1]), tok),
            pl.BlockSpec((None, tm, 2 * dm), tok),
            pl.BlockSpec((None, tm, LANES), tok),
            pl.BlockSpec((None, tm, w3), tok),
            pl.BlockSpec((None, 4, tm // 4, w3), lambda i, j: (i, 0, j, 0)),
            pl.BlockSpec((None, 16, tm // 16, w3), lambda i, j: (i, 0, j, 0)),
        ],
        out_shape=[
            jax.ShapeDtypeStruct((b, s, w_a.shape[1]), BF16),
            jax.ShapeDtypeStruct((b, s, 2 * dm), BF16),
            jax.ShapeDtypeStruct((b, s, LANES), F32),
            jax.ShapeDtypeStruct((b, s, w3), BF16),
            jax.ShapeDtypeStruct((b, 4, s // 4, w3), BF16),
            jax.ShapeDtypeStruct((b, 16, s // 16, w3), BF16),
        ],
        compiler_params=_cparams(("parallel", "parallel")),
        name="in_proj",
    )(x, norm_w, w_a, w_gate, w_ab, *w_grp)


def _dn_prep_kernel(x_ref, xp_ref, xn_ref, cw_ref, ab_ref, alog_ref, dtb_ref, hm_ref, tri_ref, place_ref,
                    shift_ref, q_ref, k_ref, v_ref, cp_ref):
    j = pl.program_id(1)
    nj = pl.num_programs(1)
    xb = x_ref[...]
    x = xb.astype(F32)
    t = x.shape[0]
    prev_row = xp_ref[BF16_ROWS - 1:BF16_ROWS, :].astype(F32) * jnp.where(j > 0, 1.0, 0.0)
    next_row = xn_ref[0:1, :].astype(F32) * jnp.where(j < nj - 1, 1.0, 0.0)
    row = _iota((t, 1), 0)
    up = jnp.where(row == 0, prev_row, _dot(shift_ref[0], xb))
    dn = jnp.where(row == t - 1, next_row, _dot(shift_ref[1], xb))
    cw = cw_ref[...]
    y = _silu(cw[0:1] * up + cw[1:2] * x + cw[2:3] * dn)
    q = y[:, 0:WIDTH]
    k = y[:, WIDTH:2 * WIDTH]
    hm = hm_ref[...]
    q_ref[...] = (q * lax.rsqrt(_head_meansq(q, hm) * HEAD_DIM + EPS)).astype(BF16)
    k_ref[...] = (k * lax.rsqrt(_head_meansq(k, hm) * HEAD_DIM + EPS)).astype(BF16)
    v_ref[...] = y[:, 2 * WIDTH:3 * WIDTH].astype(BF16)

    ab = ab_ref[...]
    a = ab[:, 0:16] + dtb_ref[...]
    softplus = jnp.maximum(a, 0.0) + jnp.log(1.0 + jnp.exp(-jnp.abs(a)))
    g = -jnp.exp(alog_ref[...]) * softplus
    beta = _sigmoid(ab[:, 16:32])
    lane = _iota((t, 16), 1)
    gc = jnp.where(lane < HEADS, _dot3(tri_ref[0], g), _dot3(tri_ref[1], g))
    acc = jnp.zeros((t, LANES), F32)
    for idx, piece in enumerate(_split3(gc) + _split3(beta)):
        acc = acc + _dot(piece, place_ref[idx])
    cp_ref[...] = acc.astype(BF16)


def _dn_prep_constants(tp):
    r = jnp.arange(tp)[:, None]
    c = jnp.arange(tp)[None, :]
    same = (r // CHUNK) == (c // CHUNK)
    tri = jnp.stack([same & (r >= c), same & (r <= c)]).astype(BF16)
    idx = jnp.arange(6)[:, None, None]
    place = (jnp.arange(LANES)[None, None, :] == jnp.arange(16)[None, :, None] + 16 * idx).astype(BF16)
    shift = jnp.stack([c == r - 1, c == r + 1]).astype(BF16)
    return tri, place, shift


def _dn_prep(pa, conv_w, ab, alog, dtb, hm, tp=256):
    b, s, _ = pa.shape
    w3 = 3 * WIDTH
    nblk = s // BF16_ROWS
    per = tp // BF16_ROWS
    tri, place, shift = _dn_prep_constants(tp)
    const2 = lambda i, j: (0, 0)
    const3 = lambda i, j: (0, 0, 0)
    return pl.pallas_call(
        _dn_prep_kernel,
        grid=(b, s // tp),
        in_specs=[
            pl.BlockSpec((None, tp, w3), lambda i, j: (i, j, 0)),
            pl.BlockSpec((None, BF16_ROWS, w3), lambda i, j: (i, jnp.maximum(j * per - 1, 0), 0)),
            pl.BlockSpec((None, BF16_ROWS, w3), lambda i, j: (i, jnp.minimum((j + 1) * per, nblk - 1), 0)),
            pl.BlockSpec((3, w3), lambda i, j: (0, 0)),
            pl.BlockSpec((None, tp, LANES), lambda i, j: (i, j, 0)),
            pl.BlockSpec((1, 16), const2),
            pl.BlockSpec((1, 16), const2),
            pl.BlockSpec((WIDTH, WIDTH), const2, pipeline_mode=pl.Buffered(1)),
            pl.BlockSpec((2, tp, tp), const3, pipeline_mode=pl.Buffered(1)),
            pl.BlockSpec((6, 16, LANES), const3, pipeline_mode=pl.Buffered(1)),
            pl.BlockSpec((2, tp, tp), const3, pipeline_mode=pl.Buffered(1)),
        ],
        out_specs=[pl.BlockSpec((None, tp, WIDTH), lambda i, j: (i, j, 0))] * 3
        + [pl.BlockSpec((None, tp, LANES), lambda i, j: (i, j, 0))],
        out_shape=[jax.ShapeDtypeStruct((b, s, WIDTH), BF16)] * 3
        + [jax.ShapeDtypeStruct((b, s, LANES), BF16)],
        compiler_params=_cparams(("parallel", "parallel")),
        name="dn_prep",
    )(pa, pa, pa, conv_w, ab, alog, dtb, hm, tri, place, shift)


def _block_diag(x):
    blk = _div(_iota(x.shape, 1), HEAD_DIM)
    parts = [jnp.where(blk == a, x, 0.0) for a in range(4)]
    return jnp.concatenate(parts, axis=0).astype(BF16)


def _dn_scan_kernel(q_ref, k_ref, v_ref, cp_ref, e2_ref, o_ref,
                    s_scr, u_scr, l1_scr, l2_scr, al_scr):
    dirn = pl.program_id(1)
    seg = pl.program_id(2)
    nb = q_ref.shape[0]
    nchunk = q_ref.shape[1] // CHUNK
    fwd = dirn == 0

    @pl.when(seg == 0)
    def _():
        s_scr[...] = jnp.zeros_like(s_scr)

    row = _iota((CHUNK, QUAD), 0)
    col = _mod(_iota((CHUNK, QUAD), 1), HEAD_DIM)
    later = jnp.where(fwd, row, col)
    earlier = jnp.where(fwd, col, row)
    incl = later >= earlier
    strict = later > earlier
    diag = row == col
    eye = diag.astype(BF16)
    unroll = min(nchunk, max(1, PHASE_A_PROBLEMS // (2 * nb)))
    assert nchunk % unroll == 0, (nchunk, unroll)

    def phase_a(it, carry):
        probs = []
        for u in range(unroll):
            c = it * unroll + u
            rows = pl.ds(pl.multiple_of(c * CHUNK, CHUNK), CHUNK)
            x_all = _dot(jnp.concatenate([cp_ref[bi, rows, :] for bi in range(nb)], axis=0), e2_ref[...])
            for bi in range(nb):
                x = x_all[bi * CHUNK:(bi + 1) * CHUNK]
                for qd in range(2):
                    lanes = slice(qd * QUAD, (qd + 1) * QUAD)
                    probs.append(dict(
                        c=c, slot=2 * bi + qd, gcol=x[:, qd * QUAD:(qd + 1) * QUAD],
                        beta=x[:, WIDTH + qd * QUAD:WIDTH + (qd + 1) * QUAD],
                        kq=k_ref[bi, rows, lanes], qq=q_ref[bi, rows, lanes], vq=v_ref[bi, rows, lanes]))
        for pr in probs:
            pr["kf"] = pr["kq"].astype(F32)
            pr["z"] = _dot_nt(jnp.concatenate([pr["kq"], pr["qq"], eye], axis=0),
                              _block_diag(pr["kf"]))
        for pr in probs:
            gcol, beta, z = pr["gcol"], pr["beta"], pr["z"]
            grow = jnp.sum(jnp.where(diag, gcol, 0.0), axis=0, keepdims=True)
            glast = jnp.where(fwd, gcol[CHUNK - 1:CHUNK, :], gcol[0:1, :])
            dm = jnp.exp(jnp.where(incl, gcol - grow, NEG_INF))
            eg = jnp.exp(gcol)
            a = jnp.where(strict, beta * z[0:CHUNK] * dm, 0.0)
            qkm = jnp.where(incl, z[CHUNK:2 * CHUNK] * dm, 0.0) * 0.125
            kdt = z[2 * CHUNK:3 * CHUNK] * jnp.exp(glast - grow)
            pr["vb"] = pr["vq"].astype(F32) * beta
            pr["kbg"] = pr["kf"] * (beta * eg)
            qdec = pr["qq"].astype(F32) * (eg * 0.125)
            c, slot = pr["c"], pr["slot"]
            l2_scr[slot, c] = jnp.concatenate([qkm, kdt], axis=0).astype(BF16)
            l1_scr[slot, c, CHUNK:2 * CHUNK, :] = qdec.astype(BF16)
            al_scr[slot, c] = jnp.broadcast_to(jnp.exp(glast), (8, QUAD))
            pr["a"] = a
        for pr in probs:
            pr["p"] = _dot(pr["a"].astype(BF16), _block_diag(pr["a"]))
            pr["n"] = -pr["a"]
        for _ in range(4):
            for pr in probs:
                n, p = pr["n"], pr["p"]
                zz = _dot(jnp.concatenate([n, p], axis=0).astype(BF16), _block_diag(p))
                pr["n"] = n + p + zz[0:CHUNK]
                pr["p"] = zz[CHUNK:2 * CHUNK]
        for pr in probs:
            n, p = pr["n"], pr["p"]
            pr["n"] = n + p + _dot(n.astype(BF16), _block_diag(p))
        for pr in probs:
            vb, kbg, c, slot = pr["vb"], pr["kbg"], pr["c"], pr["slot"]
            rhs = jnp.concatenate([_block_diag(vb), _block_diag(kbg)], axis=1)
            uw = _dot(pr["n"].astype(BF16), rhs)
            u_scr[slot, c] = vb + uw[:, 0:QUAD]
            l1_scr[slot, c, 0:CHUNK, :] = (kbg + uw[:, QUAD:2 * QUAD]).astype(BF16)
        return carry

    lax.fori_loop(0, nchunk // unroll, phase_a, 0)

    nslot = 2 * nb

    def phase_b(i, states):
        c = jnp.where(fwd, i, nchunk - 1 - i)
        r0 = pl.multiple_of(c * CHUNK, CHUNK)
        z1 = [_dot(l1_scr[sl, c], _block_diag(states[sl])) for sl in range(nslot)]
        vn = [u_scr[sl, c] - z1[sl][0:CHUNK] for sl in range(nslot)]
        z2 = [_dot(l2_scr[sl, c], _block_diag(vn[sl])) for sl in range(nslot)]
        new = []
        for sl in range(nslot):
            o = z1[sl][CHUNK:2 * CHUNK] + z2[sl][0:CHUNK]
            o_ref[sl // 2, pl.ds(r0, CHUNK), (sl % 2) * QUAD:(sl % 2 + 1) * QUAD] = o.astype(o_ref.dtype)
            new.append(al_scr[sl, c][0:1, :] * states[sl] + z2[sl][CHUNK:2 * CHUNK])
        return tuple(new)

    final = lax.fori_loop(0, nchunk, phase_b, tuple(s_scr[sl] for sl in range(nslot)))
    for sl in range(nslot):
        s_scr[sl] = final[sl]


def _dn_scan(q, k, v, cp, e2, nb=8, seg=256):
    b, s, _ = q.shape
    while b % nb:
        nb //= 2
    seg = min(seg, s)
    nseg = s // seg
    nchunk = seg // CHUNK

    def tok(i, d, j):
        return (i, jnp.where(d == 0, j, nseg - 1 - j), 0)

    return pl.pallas_call(
        _dn_scan_kernel,
        grid=(b // nb, 2, nseg),
        in_specs=[pl.BlockSpec((nb, seg, WIDTH), tok)] * 3
        + [pl.BlockSpec((nb, seg, LANES), tok),
           pl.BlockSpec((None, LANES, 2 * WIDTH), lambda i, d, j: (d, 0, 0))],
        out_specs=pl.BlockSpec((None, nb, seg, WIDTH),
                               lambda i, d, j: (d, i, jnp.where(d == 0, j, nseg - 1 - j), 0)),
        out_shape=jax.ShapeDtypeStruct((2, b, s, WIDTH), BF16),
        scratch_shapes=[
            pltpu.VMEM((2 * nb, CHUNK, QUAD), F32),
            pltpu.VMEM((2 * nb, nchunk, CHUNK, QUAD), F32),
            pltpu.VMEM((2 * nb, nchunk, 2 * CHUNK, QUAD), BF16),
            pltpu.VMEM((2 * nb, nchunk, 2 * CHUNK, QUAD), BF16),
            pltpu.VMEM((2 * nb, nchunk, 8, QUAD), F32),
        ],
        compiler_params=_cparams(("parallel", "arbitrary", "arbitrary")),
        name="dn_scan",
    )(q, k, v, cp, e2)


def _dn_expand_matrix():
    r = jnp.arange(LANES)[:, None]
    c = jnp.arange(2 * WIDTH)[None, :]
    piece, lane = r // 16, r % 16
    out = []
    for d in range(2):
        head = lane - 8 * d
        ok = (piece < 6) & (head >= 0) & (head < HEADS)
        ok = ok & ((c // WIDTH) == (piece // 3)) & (((c % WIDTH) // HEAD_DIM) == head)
        out.append(ok)
    return jnp.stack(out).astype(BF16)


QBLK = 2 * RADIUS
ATTN_ROWS = 1024


def _attn_kernel(q_ref, k_ref, kp_ref, kn_ref, v_ref, vp_ref, vn_ref, qw_ref, kw_ref, hm_ref, bias_ref,
                 o_ref, lse_ref, *, sub):
    for si in range(q_ref.shape[0]):
        _attn_tile(*(r.at[si] for r in (q_ref, k_ref, kp_ref, kn_ref, v_ref, vp_ref, vn_ref)),
                   qw_ref, kw_ref, hm_ref, bias_ref, o_ref.at[si], lse_ref.at[si], sub=sub)


def _attn_tile(q_ref, k_ref, kp_ref, kn_ref, v_ref, vp_ref, vn_ref, qw_ref, kw_ref, hm_ref, bias_ref,
               o_ref, lse_ref, *, sub):
    t = pl.program_id(1)
    tq = q_ref.shape[0]
    nqb = tq // QBLK
    low = _mod(_iota((1, WIDTH), 1), LANES) < HEAD_DIM
    hm = hm_ref[...]
    q = q_ref[...].astype(F32)
    qn = q * lax.rsqrt(_dot((q * q).astype(BF16), hm) + EPS) * (qw_ref[...] * (HEAD_DIM ** -0.5 * LOG2E))
    qn = qn.astype(BF16)
    zero = jnp.zeros((), BF16)
    one = jnp.ones((), BF16)
    qsel = (jnp.where(low, qn, zero), jnp.where(low, zero, qn))
    kx = jnp.concatenate([kp_ref[...], k_ref[...], kn_ref[...]], axis=0).astype(F32)
    kn = (kx * lax.rsqrt(_dot((kx * kx).astype(BF16), hm) + EPS) * kw_ref[...]).astype(BF16)
    vx = jnp.concatenate([vp_ref[...], v_ref[...], vn_ref[...]], axis=0)
    vsel = (jnp.where(low, vx, one), jnp.where(low, one, vx))
    c = _iota((QBLK, 2 * QBLK), 1)
    lane = _iota((1, LANES), 1)
    low_pair = lane < HEAD_DIM
    pairs = [slice((h // 2) * LANES, (h // 2 + 1) * LANES) for h in range(HEADS)]
    for qb in range(nqb):
        rows = slice(qb * QBLK, (qb + 1) * QBLK)
        krows = slice(qb * QBLK, qb * QBLK + 2 * QBLK)
        scores = [_dot_nt(qsel[h % 2][rows, pairs[h]], kn[krows, pairs[h]]) for h in range(HEADS)]
        edge = None
        if qb == 0 or qb == nqb - 1:
            kj = t * tq + (qb * QBLK - RADIUS) + c
            edge = jnp.where((kj >= 0) & (kj < sub), 0.0, NEG_INF)
        probs, maxes = [], []
        for h in range(HEADS):
            sc = scores[h] + bias_ref[h]
            if edge is not None:
                sc = sc + edge
            m = jnp.max(sc, axis=-1, keepdims=True)
            probs.append(jnp.exp2(sc - m).astype(BF16))
            maxes.append(m)
        outs = [_dot(probs[h], vsel[h % 2][krows, pairs[h]]) for h in range(HEADS)]
        lse_tile = jnp.zeros((QBLK, LANES), F32)
        for h0 in range(0, HEADS, 2):
            even, odd = outs[h0], outs[h0 + 1]
            num = jnp.where(low_pair, even, odd)
            den = pltpu.roll(jnp.where(low_pair, odd, even), HEAD_DIM, 1)
            o_ref[rows, pairs[h0]] = (num / den).astype(o_ref.dtype)
            lse_pair = jnp.where(low_pair, maxes[h0], maxes[h0 + 1]) * LN2 + jnp.log(den)
            keep = (lane == _lse_lane(h0)) | (lane == _lse_lane(h0 + 1))
            lse_tile = lse_tile + jnp.where(keep, lse_pair, 0.0)
        lse_ref[rows, :] = lse_tile


def _alibi_bias(dil, group):
    a = jnp.arange(QBLK)[:, None]
    c = jnp.arange(2 * QBLK)[None, :]
    rel = jnp.abs(a + RADIUS - c)
    slopes = 2.0 ** (-ALIBI_MAX * (group * HEADS + jnp.arange(1, HEADS + 1, dtype=F32)) / N_ATT_HEADS)
    bias = -(slopes * LOG2E)[:, None, None] * (rel * dil).astype(F32)[None]
    return jnp.where((rel <= RADIUS)[None], bias, NEG_INF)


def _attn(qkv, qw, kw, hm, *, dil, group):
    nseq, sub, _ = qkv.shape
    tq = min(ATTN_ROWS, sub)
    ns = max(1, ATTN_ROWS // sub)
    while nseq % ns:
        ns //= 2
    nt = sub // tq
    per = tq // RADIUS
    nblk = sub // RADIUS

    def main(lb):
        return pl.BlockSpec((ns, tq, WIDTH), lambda i, j: (i, j, lb))

    def prev(lb):
        return pl.BlockSpec((ns, RADIUS, WIDTH), lambda i, j: (i, jnp.maximum(j * per - 1, 0), lb))

    def nxt(lb):
        return pl.BlockSpec((ns, RADIUS, WIDTH),
                            lambda i, j: (i, jnp.minimum((j + 1) * per, nblk - 1), lb))

    wspec = pl.BlockSpec((1, WIDTH), lambda i, j: (0, 0))
    hmspec = pl.BlockSpec((WIDTH, WIDTH), lambda i, j: (0, 0), pipeline_mode=pl.Buffered(1))
    bspec = pl.BlockSpec((HEADS, QBLK, 2 * QBLK), lambda i, j: (0, 0, 0), pipeline_mode=pl.Buffered(1))
    return pl.pallas_call(
        functools.partial(_attn_kernel, sub=sub),
        grid=(nseq // ns, nt),
        in_specs=[main(0), main(1), prev(1), nxt(1), main(2), prev(2), nxt(2), wspec, wspec, hmspec, bspec],
        out_specs=[pl.BlockSpec((ns, tq, WIDTH), lambda i, j: (i, j, 0)),
                   pl.BlockSpec((ns, tq, LANES), lambda i, j: (i, j, 0))],
        out_shape=[jax.ShapeDtypeStruct((nseq, sub, WIDTH), BF16),
                   jax.ShapeDtypeStruct((nseq, sub, LANES), F32)],
        compiler_params=_cparams(("parallel", "parallel")),
        name=f"attn_g{group}",
    )(qkv, qkv, qkv, qkv, qkv, qkv, qkv, qw, kw, hm, _alibi_bias(dil, group))


def _merge_kernel(of_ref, ob_ref, z_ref, o0_ref, o1_ref, o2_ref, l0_ref, l1_ref, l2_ref,
                  gate_ref, x_ref, wdn_ref, wat_ref, wo_ref, dnw_ref, n2w_ref, hm_ref, pinv_ref, expand_ref,
                  x1_ref, h2_ref):
    tm = x_ref.shape[0]
    dm = x_ref.shape[1]
    sub = pinv_ref.shape[1]
    expand = expand_ref[...]
    blocks = [dict(rows=slice(t * sub, (t + 1) * sub), t=t) for t in range(tm // sub)]
    for bk in blocks:
        rows, t = bk["rows"], bk["t"]
        oa = of_ref[rows, :].astype(F32) + ob_ref[rows, :].astype(F32)
        bk["oa"] = oa
        bk["msq"] = _head_meansq(oa, hm_ref[...])
        outs = [o0_ref[rows, :].astype(F32)]
        lses = [l0_ref[rows, :]]
        for idx, (d, oref, lref) in enumerate(((4, o1_ref, l1_ref), (16, o2_ref, l2_ref))):
            pinv = pinv_ref[idx]
            run = sub // d
            part = slice(t * run, (t + 1) * run)
            outs.append(_dot(pinv, jnp.concatenate([oref[r, part, :] for r in range(d)], axis=0)))
            lflat = jnp.concatenate([lref[r, part, :] for r in range(d)], axis=0)
            lp = _dot(pinv, jnp.concatenate(_split3(lflat), axis=1))
            lses.append(lp[:, 0:LANES] + lp[:, LANES:2 * LANES] + lp[:, 2 * LANES:3 * LANES])
        bk["outs"], bk["lses"] = outs, lses
    for bk in blocks:
        rows = bk["rows"]
        z = z_ref[rows, :].astype(F32)
        gated = bk["oa"] * lax.rsqrt(bk["msq"] + EPS) * dnw_ref[...] * _silu(z)
        bk["ya"] = _dot(gated.astype(BF16), wdn_ref[...])
        lses = bk["lses"]
        m = jnp.maximum(jnp.maximum(lses[0], lses[1]), lses[2])
        es = [jnp.exp(l - m) for l in lses]
        den = es[0] + es[1] + es[2]
        ob = jnp.zeros((sub, WIDTH), F32)
        for e, o in zip(es, bk["outs"]):
            hi, mid, _ = _split3(e / den)
            ob = ob + _dot(jnp.concatenate([hi, mid], axis=1), expand) * o
        bk["ob"] = ob
    for bk in blocks:
        bk["yb"] = _dot(bk["ob"].astype(BF16), wat_ref[...])
    for bk in blocks:
        rows = bk["rows"]
        g = gate_ref[rows, :].astype(F32)
        mixed = _sigmoid(g[:, 0:dm]) * bk["ya"] + _sigmoid(g[:, dm:2 * dm]) * bk["yb"]
        x1 = x_ref[rows, :] + _dot(mixed.astype(BF16), wo_ref[...])
        x1_ref[rows, :] = x1
        ms = jnp.mean(x1 * x1, axis=-1, keepdims=True)
        h2_ref[rows, :] = (x1 * lax.rsqrt(ms + EPS) * n2w_ref[...]).astype(BF16)


def _merge(o_dn, pa, attn_o, attn_l, gates, x, wdn, wat, wo, dnw, n2w, hm, tm=2 * PERM_TILE):
    b, s, dm = x.shape
    tok = lambda i, j: (i, j, 0)
    const = lambda i, j: (0, 0)
    pinv = jnp.stack([_perm_matrix(PERM_TILE, 4, inverse=True), _perm_matrix(PERM_TILE, 16, inverse=True)])
    in_specs = [
        pl.BlockSpec((None, None, tm, WIDTH), lambda i, j: (0, i, j, 0)),
        pl.BlockSpec((None, None, tm, WIDTH), lambda i, j: (1, i, j, 0)),
        pl.BlockSpec((None, tm, WIDTH), lambda i, j: (i, j, 3)),
        pl.BlockSpec((None, tm, WIDTH), tok),
        pl.BlockSpec((None, 4, tm // 4, WIDTH), lambda i, j: (i, 0, j, 0)),
        pl.BlockSpec((None, 16, tm // 16, WIDTH), lambda i, j: (i, 0, j, 0)),
        pl.BlockSpec((None, tm, LANES), tok),
        pl.BlockSpec((None, 4, tm // 4, LANES), lambda i, j: (i, 0, j, 0)),
        pl.BlockSpec((None, 16, tm // 16, LANES), lambda i, j: (i, 0, j, 0)),
        pl.BlockSpec((None, tm, 2 * dm), tok),
        pl.BlockSpec((None, tm, dm), tok),
        pl.BlockSpec((WIDTH, dm), const),
        pl.BlockSpec((WIDTH, dm), const),
        pl.BlockSpec((dm, dm), const),
        pl.BlockSpec((1, WIDTH), const),
        pl.BlockSpec((1, dm), const),
        pl.BlockSpec((WIDTH, WIDTH), const, pipeline_mode=pl.Buffered(1)),
        pl.BlockSpec((2, PERM_TILE, PERM_TILE), lambda i, j: (0, 0, 0), pipeline_mode=pl.Buffered(1)),
        pl.BlockSpec((2 * LANES, WIDTH), const, pipeline_mode=pl.Buffered(1)),
    ]
    expand = _head_expand()
    return pl.pallas_call(
        _merge_kernel,
        grid=(b, s // tm),
        in_specs=in_specs,
        out_specs=[pl.BlockSpec((None, tm, dm), tok), pl.BlockSpec((None, tm, dm), tok)],
        out_shape=[jax.ShapeDtypeStruct((b, s, dm), F32), jax.ShapeDtypeStruct((b, s, dm), BF16)],
        compiler_params=_cparams(("parallel", "parallel")),
        name="merge",
    )(o_dn, o_dn, pa, attn_o[0], attn_o[1], attn_o[2], attn_l[0], attn_l[1], attn_l[2],
      gates, x, wdn, wat, wo, dnw, n2w, hm, pinv, jnp.concatenate([expand, expand], axis=0))


def _ffn_kernel(h_ref, hp_ref, hn_ref, wup_ref, cw_ref, wd_ref, x1_ref, o_ref,
                ug_ref, uu_ref, *, tiles_per_seq, chunks):
    i = pl.program_id(0)
    tm = h_ref.shape[0]
    dff = wd_ref.shape[0]
    pos = _mod(i, tiles_per_seq)
    hp = hp_ref[...] * jnp.where(pos > 0, 1.0, 0.0).astype(BF16)
    hn = hn_ref[...] * jnp.where(pos < tiles_per_seq - 1, 1.0, 0.0).astype(BF16)
    lhs = jnp.concatenate([hp, h_ref[...], hn], axis=0)

    def conv(c0, c1, ue_ref):
        w = c1 - c0
        cw = cw_ref[:, c0:c1]
        return (cw[0:1] * ue_ref[BF16_ROWS - 1:BF16_ROWS - 1 + tm, 0:w]
                + cw[1:2] * ue_ref[BF16_ROWS:BF16_ROWS + tm, 0:w]
                + cw[2:3] * ue_ref[BF16_ROWS + 1:BF16_ROWS + 1 + tm, 0:w])

    for idx, (c0, c1) in enumerate(chunks):
        ug_ref[idx, :, 0:c1 - c0] = _dot(lhs, wup_ref[:, c0:c1])
        uu_ref[idx, :, 0:c1 - c0] = _dot(lhs, wup_ref[:, dff + c0:dff + c1])
    out = x1_ref[...]
    for idx, (c0, c1) in enumerate(chunks):
        act = _silu(conv(c0, c1, ug_ref.at[idx])) * conv(dff + c0, dff + c1, uu_ref.at[idx])
        out = out + _dot(act.astype(BF16), wd_ref[c0:c1, :])
    o_ref[...] = out


MXU_TILE = 256
FFN_CHUNK_TILES = 4


def _ffn(h2, x1, w_up, conv_w, w_down, seq, tm=512):
    n, dm = h2.shape
    dff = w_down.shape[0]
    split = FFN_CHUNK_TILES * MXU_TILE
    chunks = tuple((c0, min(c0 + split, dff)) for c0 in range(0, dff, split))
    per = tm // BF16_ROWS
    nblk = n // BF16_ROWS
    const = lambda i: (0, 0)

    def resident(shape):
        return pl.BlockSpec(shape, const, pipeline_mode=pl.Buffered(1))

    return pl.pallas_call(
        functools.partial(_ffn_kernel, tiles_per_seq=seq // tm, chunks=chunks),
        grid=(n // tm,),
        in_specs=[
            pl.BlockSpec((tm, dm), lambda i: (i, 0)),
            pl.BlockSpec((BF16_ROWS, dm), lambda i: (jnp.maximum(i * per - 1, 0), 0)),
            pl.BlockSpec((BF16_ROWS, dm), lambda i: (jnp.minimum((i + 1) * per, nblk - 1), 0)),
            resident(w_up.shape),
            resident(conv_w.shape),
            resident(w_down.shape),
            pl.BlockSpec((tm, dm), lambda i: (i, 0)),
        ],
        out_specs=pl.BlockSpec((tm, dm), lambda i: (i, 0)),
        out_shape=jax.ShapeDtypeStruct((n, dm), F32),
        scratch_shapes=[pltpu.VMEM((len(chunks), tm + 2 * BF16_ROWS, split), F32),
                        pltpu.VMEM((len(chunks), tm + 2 * BF16_ROWS, split), F32)],
        compiler_params=_cparams(("parallel",)),
        name="ffn",
    )(h2, h2, h2, w_up, conv_w, w_down, x1)


def _layer(x, norm1_w, w_in, dn_conv_w, dn_a_log, dn_dt_bias, dn_out_norm_w, attn_q_norm_w,
           attn_k_norm_w, w_dn_out, w_attn_out, w_o, norm2_w, w_ffn_up, ffn_conv_w, w_ffn_down):
    b, s, dm = x.shape
    n = b * s
    w3 = 3 * WIDTH
    assert s % (max(DILATIONS) * QBLK) == 0 and s % (2 * PERM_TILE) == 0, s
    assert w_in.shape == (dm, 4 * w3 + WIDTH + 32 + 2 * dm), w_in.shape
    c_z = w3
    c_ab = c_z + WIDTH
    c_q = c_ab + 32
    c_k = c_q + w3
    c_v = c_k + w3
    c_gate = c_v + w3
    wb = w_in.astype(BF16)
    w_a = wb[:, 0:c_ab]
    w_ab = jnp.pad(wb[:, c_ab:c_q], ((0, 0), (0, LANES - 32)))
    w_grp = [jnp.concatenate([wb[:, c0 + g * WIDTH:c0 + (g + 1) * WIDTH] for c0 in (c_q, c_k, c_v)], axis=1)
             for g in range(3)]
    w_gate = wb[:, c_gate:c_gate + 2 * dm]

    hm = _head_mean_matrix()

    pa, gates, ab, qkv0, qkv1, qkv2 = _in_proj(x, norm1_w.reshape(1, dm), w_a, w_gate, w_ab, w_grp)

    q, k, v, cp = _dn_prep(pa, dn_conv_w, ab, dn_a_log.reshape(1, 16), dn_dt_bias.reshape(1, 16), hm)
    o_dn = _dn_scan(q, k, v, cp, _dn_expand_matrix())

    qw = jnp.tile(attn_q_norm_w, HEADS).reshape(1, WIDTH)
    kw = jnp.tile(attn_k_norm_w, HEADS).reshape(1, WIDTH)
    attn_o, attn_l = [], []
    for g, (d, qkv) in enumerate(zip(DILATIONS, (qkv0, qkv1, qkv2))):
        o, l = _attn(qkv.reshape(b * d, s // d, w3), qw, kw, hm, dil=d, group=g)
        shape = (b, s) if d == 1 else (b, d, s // d)
        attn_o.append(o.reshape(shape + (WIDTH,)))
        attn_l.append(l.reshape(shape + (LANES,)))

    x1, h2 = _merge(o_dn, pa, attn_o, attn_l, gates, x, w_dn_out.astype(BF16), w_attn_out.astype(BF16),
                    w_o.astype(BF16), jnp.tile(dn_out_norm_w, HEADS).reshape(1, WIDTH),
                    norm2_w.reshape(1, dm), hm)
    out = _ffn(h2.reshape(n, dm), x1.reshape(n, dm), w_ffn_up.astype(BF16), ffn_conv_w,
               w_ffn_down.astype(BF16), s)
    return out.reshape(b, s, dm)


def kernel(x, norm1_w, w_in, dn_conv_w, dn_a_log, dn_dt_bias, dn_out_norm_w, attn_q_norm_w, attn_k_norm_w, w_dn_out, w_attn_out, w_o, norm2_w, w_ffn_up, ffn_conv_w, w_ffn_down):
    for layer in range(norm1_w.shape[0]):
        x = _layer(x, norm1_w[layer], w_in[layer], dn_conv_w[layer], dn_a_log[layer], dn_dt_bias[layer],
                   dn_out_norm_w[layer], attn_q_norm_w[layer], attn_k_norm_w[layer], w_dn_out[layer],
                   w_attn_out[layer], w_o[layer], norm2_w[layer], w_ffn_up[layer], ffn_conv_w[layer],
                   w_ffn_down[layer])
    return x
```

```python
import functools

import jax
import jax.numpy as jnp
from jax import lax
from jax.experimental import pallas as pl
from jax.experimental.pallas import tpu as pltpu

F32 = jnp.float32
BF16 = jnp.bfloat16

EPS = 1e-6
NEG_INF = -1e30
LOG2E = 1.4426950408889634
LN2 = 0.6931471805599453

HEAD_DIM = 64
HEADS = 8
WIDTH = HEADS * HEAD_DIM
CHUNK = 64
RADIUS = 64
DILATIONS = (1, 4, 16)
ALIBI_MAX = 8.0
N_ATT_HEADS = 24
QUAD = 4 * HEAD_DIM
PHASE_A_PROBLEMS = 16
BF16_ROWS = 16
PERM_TILE = 256
LANES = 128
V7X_VMEM_LIMIT = 56 * 1024 * 1024


def _cparams(sem):
    return pltpu.CompilerParams(dimension_semantics=sem, vmem_limit_bytes=V7X_VMEM_LIMIT)


def _dot(a, b):
    return jnp.dot(a, b, preferred_element_type=F32)


def _dot_nt(a, b):
    return lax.dot_general(a, b, (((1,), (1,)), ((), ())), preferred_element_type=F32)


def _split3(x):
    hi = x.astype(BF16)
    r1 = x - hi.astype(F32)
    mid = r1.astype(BF16)
    lo = (r1 - mid.astype(F32)).astype(BF16)
    return hi, mid, lo


def _dot3(a, x):
    hi, mid, lo = _split3(x)
    return _dot(a, hi) + _dot(a, mid) + _dot(a, lo)


def _log2(n):
    assert n > 0 and n & (n - 1) == 0, n
    return n.bit_length() - 1


def _div(x, n):
    return lax.shift_right_logical(x, _log2(n))


def _mod(x, n):
    assert n & (n - 1) == 0, n
    return x & (n - 1)


def _iota(shape, dim):
    return lax.broadcasted_iota(jnp.int32, shape, dim)


def _sigmoid(x):
    return 0.5 * jnp.tanh(0.5 * x) + 0.5


def _silu(x):
    h = 0.5 * x
    return h + h * jnp.tanh(h)


def _head_mean_matrix():
    r = jnp.arange(WIDTH)[:, None] // HEAD_DIM
    c = jnp.arange(WIDTH)[None, :] // HEAD_DIM
    return jnp.where(r == c, 1.0 / HEAD_DIM, 0.0).astype(BF16)


def _head_meansq(x, hm):
    return _dot((x * x).astype(BF16), hm)


def _lse_lane(h):
    return h if h % 2 == 0 else HEAD_DIM + h


def _head_expand():
    r = _iota((LANES, WIDTH), 0)
    head = _div(_iota((LANES, WIDTH), 1), HEAD_DIM)
    lane = jnp.where(_mod(head, 2) == 0, head, head + HEAD_DIM)
    return (r == lane).astype(BF16)


def _perm_matrix(tm, d, inverse=False):
    run = tm // d
    row = _iota((tm, tm), 0)
    col = _iota((tm, tm), 1)
    if inverse:
        src = _mod(row, d) * run + _div(row, d)
    else:
        src = _mod(row, run) * d + _div(row, run)
    return (col == src).astype(BF16)


def _in_proj_kernel(x_ref, xp_ref, xn_ref, nw_ref, wa_ref, wgate_ref, wab_ref, wg0_ref, wg1_ref, wg2_ref,
                    cw_ref, hm_ref,
                    dq_ref, dk_ref, dv_ref, z_ref, gate_ref, ab_ref, q0_ref, q1_ref, q2_ref, ext_scr):
    j = pl.program_id(1)
    nj = pl.num_programs(1)

    def norm(v):
        ms = jnp.mean(v * v, axis=-1, keepdims=True)
        return (v * lax.rsqrt(ms + EPS) * nw_ref[...]).astype(BF16)

    x = x_ref[...]
    h = norm(x)
    tm = x.shape[0]
    nblk = tm // PERM_TILE
    hp = norm(xp_ref[...] * jnp.where(j > 0, 1.0, 0.0))
    hn = norm(xn_ref[...] * jnp.where(j < nj - 1, 1.0, 0.0))
    w3 = cw_ref.shape[1]
    ext_scr[...] = _dot(jnp.concatenate([hp, h, hn], axis=0), wa_ref[:, 0:w3])
    perm = {}
    for d in (4, 16):
        pm = _perm_matrix(PERM_TILE, d)
        perm[d] = jnp.concatenate(
            [_dot(pm, h[t * PERM_TILE:(t + 1) * PERM_TILE]).astype(BF16) for t in range(nblk)], axis=0)
    gate_ref[...] = _dot(h, wgate_ref[...]).astype(BF16)
    q0_ref[...] = _dot(h, wg0_ref[...]).astype(BF16)
    cw = cw_ref[...]
    y = _silu(cw[0:1] * ext_scr[BF16_ROWS - 1:BF16_ROWS - 1 + tm, :]
              + cw[1:2] * ext_scr[BF16_ROWS:BF16_ROWS + tm, :]
              + cw[2:3] * ext_scr[BF16_ROWS + 1:BF16_ROWS + 1 + tm, :])
    q = y[:, 0:WIDTH]
    k = y[:, WIDTH:2 * WIDTH]
    dv_ref[...] = y[:, 2 * WIDTH:3 * WIDTH].astype(BF16)
    hm = hm_ref[...]
    msq_q = _head_meansq(q, hm)
    msq_k = _head_meansq(k, hm)
    z_ref[...] = _dot(h, wa_ref[:, w3:w3 + WIDTH]).astype(BF16)
    ab_ref[...] = _dot(h, wab_ref[...])
    for d, w_ref, ref in ((4, wg1_ref, q1_ref), (16, wg2_ref, q2_ref)):
        run = PERM_TILE // d
        yp = _dot(perm[d], w_ref[...]).astype(BF16)
        for t in range(nblk):
            for r in range(d):
                ref[r, t * run:(t + 1) * run, :] = yp[t * PERM_TILE + r * run:t * PERM_TILE + (r + 1) * run]
    dq_ref[...] = (q * lax.rsqrt(msq_q * HEAD_DIM + EPS)).astype(BF16)
    dk_ref[...] = (k * lax.rsqrt(msq_k * HEAD_DIM + EPS)).astype(BF16)


def _in_proj(x, norm_w, w_a, w_gate, w_ab, w_grp, conv_w, hm, tm=2 * PERM_TILE):
    b, s, dm = x.shape
    w3 = 3 * WIDTH
    tok = lambda i, j: (i, j, 0)
    per = tm // BF16_ROWS
    nrow = s // BF16_ROWS

    def resident(shape):
        return pl.BlockSpec(shape, lambda i, j: (0, 0), pipeline_mode=pl.Buffered(1))

    return pl.pallas_call(
        _in_proj_kernel,
        grid=(b, s // tm),
        in_specs=[pl.BlockSpec((None, tm, dm), tok),
                  pl.BlockSpec((None, BF16_ROWS, dm), lambda i, j: (i, jnp.maximum(j * per - 1, 0), 0)),
                  pl.BlockSpec((None, BF16_ROWS, dm), lambda i, j: (i, jnp.minimum((j + 1) * per, nrow - 1), 0)),
                  resident((1, dm)), resident(w_a.shape), resident(w_gate.shape), resident(w_ab.shape)]
        + [resident((dm, w3))] * 3 + [resident((3, w3)), resident((WIDTH, WIDTH))],
        out_specs=[pl.BlockSpec((None, tm, WIDTH), tok)] * 4 + [
            pl.BlockSpec((None, tm, 2 * dm), tok),
            pl.BlockSpec((None, tm, LANES), tok),
            pl.BlockSpec((None, tm, w3), tok),
            pl.BlockSpec((None, 4, tm // 4, w3), lambda i, j: (i, 0, j, 0)),
            pl.BlockSpec((None, 16, tm // 16, w3), lambda i, j: (i, 0, j, 0)),
        ],
        out_shape=[jax.ShapeDtypeStruct((b, s, WIDTH), BF16)] * 4 + [
            jax.ShapeDtypeStruct((b, s, 2 * dm), BF16),
            jax.ShapeDtypeStruct((b, s, LANES), F32),
            jax.ShapeDtypeStruct((b, s, w3), BF16),
            jax.ShapeDtypeStruct((b, 4, s // 4, w3), BF16),
            jax.ShapeDtypeStruct((b, 16, s // 16, w3), BF16),
        ],
        scratch_shapes=[pltpu.VMEM((tm + 2 * BF16_ROWS, w3), F32)],
        compiler_params=_cparams(("parallel", "parallel")),
        name="in_proj",
    )(x, x, x, norm_w, w_a, w_gate, w_ab, *w_grp, conv_w, hm)


def _dn_gates_kernel(ab_ref, alog_ref, dtb_ref, tri_ref, place_ref, cp_ref):
    t = tri_ref.shape[1]
    blocks = [slice(r0, r0 + t) for r0 in range(0, ab_ref.shape[0], t)]
    lane = _iota((t, 16), 1)
    gs, betas = [], []
    for rows in blocks:
        ab = ab_ref[rows, :]
        a = ab[:, 0:16] + dtb_ref[...]
        softplus = jnp.maximum(a, 0.0) + jnp.log(1.0 + jnp.exp(-jnp.abs(a)))
        gs.append(_split3(-jnp.exp(alog_ref[...]) * softplus))
        betas.append(_sigmoid(ab[:, 16:32]))
    fwd = [[_dot(tri_ref[0], p) for p in g3] for g3 in gs]
    bwd = [[_dot(tri_ref[1], p) for p in g3] for g3 in gs]
    pieces = [_split3(jnp.where(lane < HEADS, f[0] + f[1] + f[2], r[0] + r[1] + r[2])) + _split3(beta)
              for f, r, beta in zip(fwd, bwd, betas)]
    placed = [[_dot(piece, place_ref[idx]) for idx, piece in enumerate(ps)] for ps in pieces]
    for rows, pl6 in zip(blocks, placed):
        cp_ref[rows, :] = (pl6[0] + pl6[1] + pl6[2] + pl6[3] + pl6[4] + pl6[5]).astype(BF16)


def _dn_gates_constants(tp):
    r = jnp.arange(tp)[:, None]
    c = jnp.arange(tp)[None, :]
    same = (r // CHUNK) == (c // CHUNK)
    tri = jnp.stack([same & (r >= c), same & (r <= c)]).astype(BF16)
    idx = jnp.arange(6)[:, None, None]
    place = (jnp.arange(LANES)[None, None, :] == jnp.arange(16)[None, :, None] + 16 * idx).astype(BF16)
    return tri, place


def _dn_gates(ab, alog, dtb, tp=1024, blk=256):
    b, s, _ = ab.shape
    tp = min(tp, s)
    tri, place = _dn_gates_constants(blk)
    const2 = lambda i, j: (0, 0)
    const3 = lambda i, j: (0, 0, 0)
    return pl.pallas_call(
        _dn_gates_kernel,
        grid=(b, s // tp),
        in_specs=[
            pl.BlockSpec((None, tp, LANES), lambda i, j: (i, j, 0)),
            pl.BlockSpec((1, 16), const2),
            pl.BlockSpec((1, 16), const2),
            pl.BlockSpec((2, blk, blk), const3, pipeline_mode=pl.Buffered(1)),
            pl.BlockSpec((6, 16, LANES), const3, pipeline_mode=pl.Buffered(1)),
        ],
        out_specs=pl.BlockSpec((None, tp, LANES), lambda i, j: (i, j, 0)),
        out_shape=jax.ShapeDtypeStruct((b, s, LANES), BF16),
        compiler_params=_cparams(("parallel", "parallel")),
        name="dn_gates",
    )(ab, alog, dtb, tri, place)


def _block_diag(x):
    blk = _div(_iota(x.shape, 1), HEAD_DIM)
    parts = [jnp.where(blk == a, x, 0.0) for a in range(4)]
    return jnp.concatenate(parts, axis=0).astype(BF16)


def _dn_scan_kernel(q_ref, k_ref, v_ref, cp_ref, e2_ref, o_ref,
                    s_scr, u_scr, l1_scr, l2_scr, al_scr):
    dirn = pl.program_id(1)
    seg = pl.program_id(2)
    nb = q_ref.shape[0]
    nchunk = q_ref.shape[1] // CHUNK
    fwd = dirn == 0

    @pl.when(seg == 0)
    def _():
        s_scr[...] = jnp.zeros_like(s_scr)

    row = _iota((CHUNK, QUAD), 0)
    col = _mod(_iota((CHUNK, QUAD), 1), HEAD_DIM)
    later = jnp.where(fwd, row, col)
    earlier = jnp.where(fwd, col, row)
    incl = later >= earlier
    strict = later > earlier
    diag = row == col
    eye = diag.astype(BF16)
    unroll = min(nchunk, max(1, PHASE_A_PROBLEMS // (2 * nb)))
    assert nchunk % unroll == 0, (nchunk, unroll)

    def phase_a(it, carry):
        probs = []
        for u in range(unroll):
            c = it * unroll + u
            rows = pl.ds(pl.multiple_of(c * CHUNK, CHUNK), CHUNK)
            x_all = _dot(jnp.concatenate([cp_ref[bi, rows, :] for bi in range(nb)], axis=0), e2_ref[...])
            for bi in range(nb):
                x = x_all[bi * CHUNK:(bi + 1) * CHUNK]
                for qd in range(2):
                    lanes = slice(qd * QUAD, (qd + 1) * QUAD)
                    probs.append(dict(
                        c=c, slot=2 * bi + qd, gcol=x[:, qd * QUAD:(qd + 1) * QUAD],
                        beta=x[:, WIDTH + qd * QUAD:WIDTH + (qd + 1) * QUAD],
                        kq=k_ref[bi, rows, lanes], qq=q_ref[bi, rows, lanes], vq=v_ref[bi, rows, lanes]))
        for pr in probs:
            pr["kf"] = pr["kq"].astype(F32)
            pr["z"] = _dot_nt(jnp.concatenate([pr["kq"], pr["qq"], eye], axis=0),
                              _block_diag(pr["kf"]))
        for pr in probs:
            gcol, beta, z = pr["gcol"], pr["beta"], pr["z"]
            grow = jnp.sum(jnp.where(diag, gcol, 0.0), axis=0, keepdims=True)
            glast = jnp.where(fwd, gcol[CHUNK - 1:CHUNK, :], gcol[0:1, :])
            dm = jnp.exp(jnp.where(incl, gcol - grow, NEG_INF))
            eg = jnp.exp(gcol)
            a = jnp.where(strict, beta * z[0:CHUNK] * dm, 0.0)
            qkm = jnp.where(incl, z[CHUNK:2 * CHUNK] * dm, 0.0) * 0.125
            kdt = z[2 * CHUNK:3 * CHUNK] * jnp.exp(glast - grow)
            pr["vb"] = pr["vq"].astype(F32) * beta
            pr["kbg"] = pr["kf"] * (beta * eg)
            qdec = pr["qq"].astype(F32) * (eg * 0.125)
            c, slot = pr["c"], pr["slot"]
            l2_scr[slot, c] = jnp.concatenate([qkm, kdt], axis=0).astype(BF16)
            l1_scr[slot, c, CHUNK:2 * CHUNK, :] = qdec.astype(BF16)
            al_scr[slot, c] = jnp.broadcast_to(jnp.exp(glast), (8, QUAD))
            pr["a"] = a
        for pr in probs:
            pr["p"] = _dot(pr["a"].astype(BF16), _block_diag(pr["a"]))
            pr["n"] = -pr["a"]
        for _ in range(4):
            for pr in probs:
                n, p = pr["n"], pr["p"]
                zz = _dot(jnp.concatenate([n, p], axis=0).astype(BF16), _block_diag(p))
                pr["n"] = n + p + zz[0:CHUNK]
                pr["p"] = zz[CHUNK:2 * CHUNK]
        for pr in probs:
            n, p = pr["n"], pr["p"]
            pr["n"] = n + p + _dot(n.astype(BF16), _block_diag(p))
        for pr in probs:
            vb, kbg, c, slot = pr["vb"], pr["kbg"], pr["c"], pr["slot"]
            rhs = jnp.concatenate([_block_diag(vb), _block_diag(kbg)], axis=1)
            uw = _dot(pr["n"].astype(BF16), rhs)
            u_scr[slot, c] = vb + uw[:, 0:QUAD]
            l1_scr[slot, c, 0:CHUNK, :] = (kbg + uw[:, QUAD:2 * QUAD]).astype(BF16)
        return carry

    lax.fori_loop(0, nchunk // unroll, phase_a, 0)

    nslot = 2 * nb

    def phase_b(i, states):
        c = jnp.where(fwd, i, nchunk - 1 - i)
        r0 = pl.multiple_of(c * CHUNK, CHUNK)
        z1 = [_dot(l1_scr[sl, c], _block_diag(states[sl])) for sl in range(nslot)]
        vn = [u_scr[sl, c] - z1[sl][0:CHUNK] for sl in range(nslot)]
        z2 = [_dot(l2_scr[sl, c], _block_diag(vn[sl])) for sl in range(nslot)]
        new = []
        for sl in range(nslot):
            o = z1[sl][CHUNK:2 * CHUNK] + z2[sl][0:CHUNK]
            o_ref[sl // 2, pl.ds(r0, CHUNK), (sl % 2) * QUAD:(sl % 2 + 1) * QUAD] = o.astype(o_ref.dtype)
            new.append(al_scr[sl, c][0:1, :] * states[sl] + z2[sl][CHUNK:2 * CHUNK])
        return tuple(new)

    final = lax.fori_loop(0, nchunk, phase_b, tuple(s_scr[sl] for sl in range(nslot)))
    for sl in range(nslot):
        s_scr[sl] = final[sl]


def _dn_scan(q, k, v, cp, e2, nb=8, seg=256):
    b, s, _ = q.shape
    while b % nb:
        nb //= 2
    seg = min(seg, s)
    nseg = s // seg
    nchunk = seg // CHUNK

    def tok(i, d, j):
        return (i, jnp.where(d == 0, j, nseg - 1 - j), 0)

    return pl.pallas_call(
        _dn_scan_kernel,
        grid=(b // nb, 2, nseg),
        in_specs=[pl.BlockSpec((nb, seg, WIDTH), tok)] * 3
        + [pl.BlockSpec((nb, seg, LANES), tok),
           pl.BlockSpec((None, LANES, 2 * WIDTH), lambda i, d, j: (d, 0, 0))],
        out_specs=pl.BlockSpec((None, nb, seg, WIDTH),
                               lambda i, d, j: (d, i, jnp.where(d == 0, j, nseg - 1 - j), 0)),
        out_shape=jax.ShapeDtypeStruct((2, b, s, WIDTH), BF16),
        scratch_shapes=[
            pltpu.VMEM((2 * nb, CHUNK, QUAD), F32),
            pltpu.VMEM((2 * nb, nchunk, CHUNK, QUAD), F32),
            pltpu.VMEM((2 * nb, nchunk, 2 * CHUNK, QUAD), BF16),
            pltpu.VMEM((2 * nb, nchunk, 2 * CHUNK, QUAD), BF16),
            pltpu.VMEM((2 * nb, nchunk, 8, QUAD), F32),
        ],
        compiler_params=_cparams(("parallel", "arbitrary", "arbitrary")),
        name="dn_scan",
    )(q, k, v, cp, e2)


def _dn_expand_matrix():
    r = jnp.arange(LANES)[:, None]
    c = jnp.arange(2 * WIDTH)[None, :]
    piece, lane = r // 16, r % 16
    out = []
    for d in range(2):
        head = lane - 8 * d
        ok = (piece < 6) & (head >= 0) & (head < HEADS)
        ok = ok & ((c // WIDTH) == (piece // 3)) & (((c % WIDTH) // HEAD_DIM) == head)
        out.append(ok)
    return jnp.stack(out).astype(BF16)


QBLK = 2 * RADIUS
ATTN_ROWS = 1024


def _attn_kernel(q_ref, k_ref, kp_ref, kn_ref, v_ref, vp_ref, vn_ref, qw_ref, kw_ref, hm_ref, bias_ref,
                 o_ref, lse_ref, *, sub):
    for si in range(q_ref.shape[0]):
        _attn_tile(*(r.at[si] for r in (q_ref, k_ref, kp_ref, kn_ref, v_ref, vp_ref, vn_ref)),
                   qw_ref, kw_ref, hm_ref, bias_ref, o_ref.at[si], lse_ref.at[si], sub=sub)


def _attn_tile(q_ref, k_ref, kp_ref, kn_ref, v_ref, vp_ref, vn_ref, qw_ref, kw_ref, hm_ref, bias_ref,
               o_ref, lse_ref, *, sub):
    t = pl.program_id(1)
    tq = q_ref.shape[0]
    nqb = tq // QBLK
    low = _mod(_iota((1, WIDTH), 1), LANES) < HEAD_DIM
    hm = hm_ref[...]
    q = q_ref[...].astype(F32)
    qn = q * lax.rsqrt(_dot((q * q).astype(BF16), hm) + EPS) * (qw_ref[...] * (HEAD_DIM ** -0.5 * LOG2E))
    qn = qn.astype(BF16)
    zero = jnp.zeros((), BF16)
    one = jnp.ones((), BF16)
    qsel = (jnp.where(low, qn, zero), jnp.where(low, zero, qn))
    kx = jnp.concatenate([kp_ref[...], k_ref[...], kn_ref[...]], axis=0).astype(F32)
    kn = (kx * lax.rsqrt(_dot((kx * kx).astype(BF16), hm) + EPS) * kw_ref[...]).astype(BF16)
    vx = jnp.concatenate([vp_ref[...], v_ref[...], vn_ref[...]], axis=0)
    vsel = (jnp.where(low, vx, one), jnp.where(low, one, vx))
    c = _iota((QBLK, 2 * QBLK), 1)
    lane = _iota((1, LANES), 1)
    low_pair = lane < HEAD_DIM
    pairs = [slice((h // 2) * LANES, (h // 2 + 1) * LANES) for h in range(HEADS)]
    for qb in range(nqb):
        rows = slice(qb * QBLK, (qb + 1) * QBLK)
        krows = slice(qb * QBLK, qb * QBLK + 2 * QBLK)
        scores = [_dot_nt(qsel[h % 2][rows, pairs[h]], kn[krows, pairs[h]]) for h in range(HEADS)]
        edge = None
        if qb == 0 or qb == nqb - 1:
            kj = t * tq + (qb * QBLK - RADIUS) + c
            edge = jnp.where((kj >= 0) & (kj < sub), 0.0, NEG_INF)
        probs, maxes = [], []
        for h in range(HEADS):
            sc = scores[h] + bias_ref[h]
            if edge is not None:
                sc = sc + edge
            m = jnp.max(sc, axis=-1, keepdims=True)
            probs.append(jnp.exp2(sc - m).astype(BF16))
            maxes.append(m)
        outs = [_dot(probs[h], vsel[h % 2][krows, pairs[h]]) for h in range(HEADS)]
        lse_tile = jnp.zeros((QBLK, LANES), F32)
        for h0 in range(0, HEADS, 2):
            even, odd = outs[h0], outs[h0 + 1]
            num = jnp.where(low_pair, even, odd)
            den = pltpu.roll(jnp.where(low_pair, odd, even), HEAD_DIM, 1)
            o_ref[rows, pairs[h0]] = (num / den).astype(o_ref.dtype)
            lse_pair = jnp.where(low_pair, maxes[h0], maxes[h0 + 1]) * LN2 + jnp.log(den)
            keep = (lane == _lse_lane(h0)) | (lane == _lse_lane(h0 + 1))
            lse_tile = lse_tile + jnp.where(keep, lse_pair, 0.0)
        lse_ref[rows, :] = lse_tile


def _alibi_bias(dil, group):
    a = jnp.arange(QBLK)[:, None]
    c = jnp.arange(2 * QBLK)[None, :]
    rel = jnp.abs(a + RADIUS - c)
    slopes = 2.0 ** (-ALIBI_MAX * (group * HEADS + jnp.arange(1, HEADS + 1, dtype=F32)) / N_ATT_HEADS)
    bias = -(slopes * LOG2E)[:, None, None] * (rel * dil).astype(F32)[None]
    return jnp.where((rel <= RADIUS)[None], bias, NEG_INF)


def _attn(qkv, qw, kw, hm, *, dil, group):
    nseq, sub, _ = qkv.shape
    tq = min(ATTN_ROWS, sub)
    ns = max(1, ATTN_ROWS // sub)
    while nseq % ns:
        ns //= 2
    nt = sub // tq
    per = tq // RADIUS
    nblk = sub // RADIUS

    def main(lb):
        return pl.BlockSpec((ns, tq, WIDTH), lambda i, j: (i, j, lb))

    def prev(lb):
        return pl.BlockSpec((ns, RADIUS, WIDTH), lambda i, j: (i, jnp.maximum(j * per - 1, 0), lb))

    def nxt(lb):
        return pl.BlockSpec((ns, RADIUS, WIDTH),
                            lambda i, j: (i, jnp.minimum((j + 1) * per, nblk - 1), lb))

    wspec = pl.BlockSpec((1, WIDTH), lambda i, j: (0, 0))
    hmspec = pl.BlockSpec((WIDTH, WIDTH), lambda i, j: (0, 0), pipeline_mode=pl.Buffered(1))
    bspec = pl.BlockSpec((HEADS, QBLK, 2 * QBLK), lambda i, j: (0, 0, 0), pipeline_mode=pl.Buffered(1))
    return pl.pallas_call(
        functools.partial(_attn_kernel, sub=sub),
        grid=(nseq // ns, nt),
        in_specs=[main(0), main(1), prev(1), nxt(1), main(2), prev(2), nxt(2), wspec, wspec, hmspec, bspec],
        out_specs=[pl.BlockSpec((ns, tq, WIDTH), lambda i, j: (i, j, 0)),
                   pl.BlockSpec((ns, tq, LANES), lambda i, j: (i, j, 0))],
        out_shape=[jax.ShapeDtypeStruct((nseq, sub, WIDTH), BF16),
                   jax.ShapeDtypeStruct((nseq, sub, LANES), F32)],
        compiler_params=_cparams(("parallel", "parallel")),
        name=f"attn_g{group}",
    )(qkv, qkv, qkv, qkv, qkv, qkv, qkv, qw, kw, hm, _alibi_bias(dil, group))


def _merge_kernel(of_ref, ob_ref, z_ref, o0_ref, o1_ref, o2_ref, l0_ref, l1_ref, l2_ref,
                  gate_ref, x_ref, wdn_ref, wat_ref, wo_ref, dnw_ref, n2w_ref, hm_ref, pinv_ref, expand_ref,
                  x1_ref, h2_ref):
    tm = x_ref.shape[0]
    dm = x_ref.shape[1]
    sub = pinv_ref.shape[1]
    expand = expand_ref[...]
    blocks = [dict(rows=slice(t * sub, (t + 1) * sub), t=t) for t in range(tm // sub)]
    for bk in blocks:
        rows, t = bk["rows"], bk["t"]
        oa = of_ref[rows, :].astype(F32) + ob_ref[rows, :].astype(F32)
        bk["oa"] = oa
        bk["msq"] = _head_meansq(oa, hm_ref[...])
        outs = [o0_ref[rows, :].astype(F32)]
        lses = [l0_ref[rows, :]]
        for idx, (d, oref, lref) in enumerate(((4, o1_ref, l1_ref), (16, o2_ref, l2_ref))):
            pinv = pinv_ref[idx]
            run = sub // d
            part = slice(t * run, (t + 1) * run)
            outs.append(_dot(pinv, jnp.concatenate([oref[r, part, :] for r in range(d)], axis=0)))
            lflat = jnp.concatenate([lref[r, part, :] for r in range(d)], axis=0)
            lp = _dot(pinv, jnp.concatenate(_split3(lflat), axis=1))
            lses.append(lp[:, 0:LANES] + lp[:, LANES:2 * LANES] + lp[:, 2 * LANES:3 * LANES])
        bk["outs"], bk["lses"] = outs, lses
    for bk in blocks:
        rows = bk["rows"]
        z = z_ref[rows, :].astype(F32)
        gated = bk["oa"] * lax.rsqrt(bk["msq"] + EPS) * dnw_ref[...] * _silu(z)
        bk["ya"] = _dot(gated.astype(BF16), wdn_ref[...])
        lses = bk["lses"]
        m = jnp.maximum(jnp.maximum(lses[0], lses[1]), lses[2])
        es = [jnp.exp(l - m) for l in lses]
        den = es[0] + es[1] + es[2]
        ob = jnp.zeros((sub, WIDTH), F32)
        for e, o in zip(es, bk["outs"]):
            hi, mid, _ = _split3(e / den)
            ob = ob + _dot(jnp.concatenate([hi, mid], axis=1), expand) * o
        bk["ob"] = ob
    for bk in blocks:
        bk["yb"] = _dot(bk["ob"].astype(BF16), wat_ref[...])
    for bk in blocks:
        rows = bk["rows"]
        g = gate_ref[rows, :].astype(F32)
        mixed = _sigmoid(g[:, 0:dm]) * bk["ya"] + _sigmoid(g[:, dm:2 * dm]) * bk["yb"]
        x1 = x_ref[rows, :] + _dot(mixed.astype(BF16), wo_ref[...])
        x1_ref[rows, :] = x1
        ms = jnp.mean(x1 * x1, axis=-1, keepdims=True)
        h2_ref[rows, :] = (x1 * lax.rsqrt(ms + EPS) * n2w_ref[...]).astype(BF16)


def _merge(o_dn, z, attn_o, attn_l, gates, x, wdn, wat, wo, dnw, n2w, hm, tm=2 * PERM_TILE):
    b, s, dm = x.shape
    tok = lambda i, j: (i, j, 0)
    const = lambda i, j: (0, 0)
    pinv = jnp.stack([_perm_matrix(PERM_TILE, 4, inverse=True), _perm_matrix(PERM_TILE, 16, inverse=True)])
    in_specs = [
        pl.BlockSpec((None, None, tm, WIDTH), lambda i, j: (0, i, j, 0)),
        pl.BlockSpec((None, None, tm, WIDTH), lambda i, j: (1, i, j, 0)),
        pl.BlockSpec((None, tm, WIDTH), tok),
        pl.BlockSpec((None, tm, WIDTH), tok),
        pl.BlockSpec((None, 4, tm // 4, WIDTH), lambda i, j: (i, 0, j, 0)),
        pl.BlockSpec((None, 16, tm // 16, WIDTH), lambda i, j: (i, 0, j, 0)),
        pl.BlockSpec((None, tm, LANES), tok),
        pl.BlockSpec((None, 4, tm // 4, LANES), lambda i, j: (i, 0, j, 0)),
        pl.BlockSpec((None, 16, tm // 16, LANES), lambda i, j: (i, 0, j, 0)),
        pl.BlockSpec((None, tm, 2 * dm), tok),
        pl.BlockSpec((None, tm, dm), tok),
        pl.BlockSpec((WIDTH, dm), const),
        pl.BlockSpec((WIDTH, dm), const),
        pl.BlockSpec((dm, dm), const),
        pl.BlockSpec((1, WIDTH), const),
        pl.BlockSpec((1, dm), const),
        pl.BlockSpec((WIDTH, WIDTH), const, pipeline_mode=pl.Buffered(1)),
        pl.BlockSpec((2, PERM_TILE, PERM_TILE), lambda i, j: (0, 0, 0), pipeline_mode=pl.Buffered(1)),
        pl.BlockSpec((2 * LANES, WIDTH), const, pipeline_mode=pl.Buffered(1)),
    ]
    expand = _head_expand()
    return pl.pallas_call(
        _merge_kernel,
        grid=(b, s // tm),
        in_specs=in_specs,
        out_specs=[pl.BlockSpec((None, tm, dm), tok), pl.BlockSpec((None, tm, dm), tok)],
        out_shape=[jax.ShapeDtypeStruct((b, s, dm), F32), jax.ShapeDtypeStruct((b, s, dm), BF16)],
        compiler_params=_cparams(("parallel", "parallel")),
        name="merge",
    )(o_dn, o_dn, z, attn_o[0], attn_o[1], attn_o[2], attn_l[0], attn_l[1], attn_l[2],
      gates, x, wdn, wat, wo, dnw, n2w, hm, pinv, jnp.concatenate([expand, expand], axis=0))


def _ffn_kernel(h_ref, hp_ref, hn_ref, wup_ref, cw_ref, wd_ref, x1_ref, o_ref,
                ug_ref, uu_ref, *, tiles_per_seq, chunks):
    i = pl.program_id(0)
    tm = h_ref.shape[0]
    dff = wd_ref.shape[0]
    pos = _mod(i, tiles_per_seq)
    hp = hp_ref[...] * jnp.where(pos > 0, 1.0, 0.0).astype(BF16)
    hn = hn_ref[...] * jnp.where(pos < tiles_per_seq - 1, 1.0, 0.0).astype(BF16)
    lhs = jnp.concatenate([hp, h_ref[...], hn], axis=0)

    def conv(c0, c1, ue_ref):
        w = c1 - c0
        cw = cw_ref[:, c0:c1]
        return (cw[0:1] * ue_ref[BF16_ROWS - 1:BF16_ROWS - 1 + tm, 0:w]
                + cw[1:2] * ue_ref[BF16_ROWS:BF16_ROWS + tm, 0:w]
                + cw[2:3] * ue_ref[BF16_ROWS + 1:BF16_ROWS + 1 + tm, 0:w])

    for idx, (c0, c1) in enumerate(chunks):
        ug_ref[idx, :, 0:c1 - c0] = _dot(lhs, wup_ref[:, c0:c1])
        uu_ref[idx, :, 0:c1 - c0] = _dot(lhs, wup_ref[:, dff + c0:dff + c1])
    out = x1_ref[...]
    for idx, (c0, c1) in enumerate(chunks):
        act = _silu(conv(c0, c1, ug_ref.at[idx])) * conv(dff + c0, dff + c1, uu_ref.at[idx])
        out = out + _dot(act.astype(BF16), wd_ref[c0:c1, :])
    o_ref[...] = out


MXU_TILE = 256
FFN_CHUNK_TILES = 4


def _ffn(h2, x1, w_up, conv_w, w_down, seq, tm=512):
    n, dm = h2.shape
    dff = w_down.shape[0]
    split = FFN_CHUNK_TILES * MXU_TILE
    chunks = tuple((c0, min(c0 + split, dff)) for c0 in range(0, dff, split))
    per = tm // BF16_ROWS
    nblk = n // BF16_ROWS
    const = lambda i: (0, 0)

    def resident(shape):
        return pl.BlockSpec(shape, const, pipeline_mode=pl.Buffered(1))

    return pl.pallas_call(
        functools.partial(_ffn_kernel, tiles_per_seq=seq // tm, chunks=chunks),
        grid=(n // tm,),
        in_specs=[
            pl.BlockSpec((tm, dm), lambda i: (i, 0)),
            pl.BlockSpec((BF16_ROWS, dm), lambda i: (jnp.maximum(i * per - 1, 0), 0)),
            pl.BlockSpec((BF16_ROWS, dm), lambda i: (jnp.minimum((i + 1) * per, nblk - 1), 0)),
            resident(w_up.shape),
            resident(conv_w.shape),
            resident(w_down.shape),
            pl.BlockSpec((tm, dm), lambda i: (i, 0)),
        ],
        out_specs=pl.BlockSpec((tm, dm), lambda i: (i, 0)),
        out_shape=jax.ShapeDtypeStruct((n, dm), F32),
        scratch_shapes=[pltpu.VMEM((len(chunks), tm + 2 * BF16_ROWS, split), F32),
                        pltpu.VMEM((len(chunks), tm + 2 * BF16_ROWS, split), F32)],
        compiler_params=_cparams(("parallel",)),
        name="ffn",
    )(h2, h2, h2, w_up, conv_w, w_down, x1)


def _layer(x, norm1_w, w_in, dn_conv_w, dn_a_log, dn_dt_bias, dn_out_norm_w, attn_q_norm_w,
           attn_k_norm_w, w_dn_out, w_attn_out, w_o, norm2_w, w_ffn_up, ffn_conv_w, w_ffn_down):
    b, s, dm = x.shape
    n = b * s
    w3 = 3 * WIDTH
    assert s % (max(DILATIONS) * QBLK) == 0 and s % (2 * PERM_TILE) == 0, s
    assert w_in.shape == (dm, 4 * w3 + WIDTH + 32 + 2 * dm), w_in.shape
    c_z = w3
    c_ab = c_z + WIDTH
    c_q = c_ab + 32
    c_k = c_q + w3
    c_v = c_k + w3
    c_gate = c_v + w3
    wb = w_in.astype(BF16)
    w_a = wb[:, 0:c_ab]
    w_ab = jnp.pad(wb[:, c_ab:c_q], ((0, 0), (0, LANES - 32)))
    w_grp = [jnp.concatenate([wb[:, c0 + g * WIDTH:c0 + (g + 1) * WIDTH] for c0 in (c_q, c_k, c_v)], axis=1)
             for g in range(3)]
    w_gate = wb[:, c_gate:c_gate + 2 * dm]

    hm = _head_mean_matrix()

    q, k, v, z, gates, ab, qkv0, qkv1, qkv2 = _in_proj(x, norm1_w.reshape(1, dm), w_a, w_gate, w_ab, w_grp,
                                                       dn_conv_w, hm)

    cp = _dn_gates(ab, dn_a_log.reshape(1, 16), dn_dt_bias.reshape(1, 16))
    o_dn = _dn_scan(q, k, v, cp, _dn_expand_matrix())

    qw = jnp.tile(attn_q_norm_w, HEADS).reshape(1, WIDTH)
    kw = jnp.tile(attn_k_norm_w, HEADS).reshape(1, WIDTH)
    attn_o, attn_l = [], []
    for g, (d, qkv) in enumerate(zip(DILATIONS, (qkv0, qkv1, qkv2))):
        o, l = _attn(qkv.reshape(b * d, s // d, w3), qw, kw, hm, dil=d, group=g)
        shape = (b, s) if d == 1 else (b, d, s // d)
        attn_o.append(o.reshape(shape + (WIDTH,)))
        attn_l.append(l.reshape(shape + (LANES,)))

    x1, h2 = _merge(o_dn, z, attn_o, attn_l, gates, x, w_dn_out.astype(BF16), w_attn_out.astype(BF16),
                    w_o.astype(BF16), jnp.tile(dn_out_norm_w, HEADS).reshape(1, WIDTH),
                    norm2_w.reshape(1, dm), hm)
    out = _ffn(h2.reshape(n, dm), x1.reshape(n, dm), w_ffn_up.astype(BF16), ffn_conv_w,
               w_ffn_down.astype(BF16), s)
    return out.reshape(b, s, dm)


def kernel(x, norm1_w, w_in, dn_conv_w, dn_a_log, dn_dt_bias, dn_out_norm_w, attn_q_norm_w, attn_k_norm_w, w_dn_out, w_attn_out, w_o, norm2_w, w_ffn_up, ffn_conv_w, w_ffn_down):
    for layer in range(norm1_w.shape[0]):
        x = _layer(x, norm1_w[layer], w_in[layer], dn_conv_w[layer], dn_a_log[layer], dn_dt_bias[layer],
                   dn_out_norm_w[layer], attn_q_norm_w[layer], attn_k_norm_w[layer], w_dn_out[layer],
                   w_attn_out[layer], w_o[layer], norm2_w[layer], w_ffn_up[layer], ffn_conv_w[layer],
                   w_ffn_down[layer])
    return x
```

```python
import functools

import jax
import jax.numpy as jnp
from jax import lax
from jax.experimental import pallas as pl
from jax.experimental.pallas import tpu as pltpu

F32 = jnp.float32
BF16 = jnp.bfloat16

EPS = 1e-6
NEG_INF = -1e30
LOG2E = 1.4426950408889634
LN2 = 0.6931471805599453

HEAD_DIM = 64
HEADS = 8
WIDTH = HEADS * HEAD_DIM
CHUNK = 64
RADIUS = 64
DILATIONS = (1, 4, 16)
ALIBI_MAX = 8.0
N_ATT_HEADS = 24
QUAD = 4 * HEAD_DIM
PHASE_A_PROBLEMS = 16
BF16_ROWS = 16
PERM_TILE = 256
LANES = 128
V7X_VMEM_LIMIT = 56 * 1024 * 1024


def _cparams(sem):
    return pltpu.CompilerParams(dimension_semantics=sem, vmem_limit_bytes=V7X_VMEM_LIMIT)


def _dot(a, b):
    return jnp.dot(a, b, preferred_element_type=F32)


def _dot_nt(a, b):
    return lax.dot_general(a, b, (((1,), (1,)), ((), ())), preferred_element_type=F32)


def _split3(x):
    hi = x.astype(BF16)
    r1 = x - hi.astype(F32)
    mid = r1.astype(BF16)
    lo = (r1 - mid.astype(F32)).astype(BF16)
    return hi, mid, lo


def _dot3(a, x):
    hi, mid, lo = _split3(x)
    return _dot(a, hi) + _dot(a, mid) + _dot(a, lo)


def _log2(n):
    assert n > 0 and n & (n - 1) == 0, n
    return n.bit_length() - 1


def _div(x, n):
    return lax.shift_right_logical(x, _log2(n))


def _mod(x, n):
    assert n & (n - 1) == 0, n
    return x & (n - 1)


def _iota(shape, dim):
    return lax.broadcasted_iota(jnp.int32, shape, dim)


def _sigmoid(x):
    return 0.5 * jnp.tanh(0.5 * x) + 0.5


def _silu(x):
    h = 0.5 * x
    return h + h * jnp.tanh(h)


def _head_mean_matrix():
    r = jnp.arange(WIDTH)[:, None] // HEAD_DIM
    c = jnp.arange(WIDTH)[None, :] // HEAD_DIM
    return jnp.where(r == c, 1.0 / HEAD_DIM, 0.0).astype(BF16)


def _head_meansq(x, hm):
    return _dot((x * x).astype(BF16), hm)


def _lse_lane(h):
    return h if h % 2 == 0 else HEAD_DIM + h


def _head_expand():
    r = _iota((LANES, WIDTH), 0)
    head = _div(_iota((LANES, WIDTH), 1), HEAD_DIM)
    lane = jnp.where(_mod(head, 2) == 0, head, head + HEAD_DIM)
    return (r == lane).astype(BF16)


def _perm_matrix(tm, d, inverse=False):
    run = tm // d
    row = _iota((tm, tm), 0)
    col = _iota((tm, tm), 1)
    if inverse:
        src = _mod(row, d) * run + _div(row, d)
    else:
        src = _mod(row, run) * d + _div(row, run)
    return (col == src).astype(BF16)


def _in_proj_kernel(x_ref, xp_ref, xn_ref, nw_ref, wa_ref, wgate_ref, wab_ref, wg0_ref, wg1_ref, wg2_ref,
                    cw_ref, hm_ref, perm_ref,
                    dq_ref, dk_ref, dv_ref, z_ref, gate_ref, ab_ref, q0_ref, q1_ref, q2_ref, ext_scr):
    j = pl.program_id(1)
    nj = pl.num_programs(1)

    def norm(v):
        ms = jnp.mean(v * v, axis=-1, keepdims=True)
        return (v * lax.rsqrt(ms + EPS) * nw_ref[...]).astype(BF16)

    x = x_ref[...]
    h = norm(x)
    tm = x.shape[0]
    nblk = tm // PERM_TILE
    hp = norm(xp_ref[...] * jnp.where(j > 0, 1.0, 0.0))
    hn = norm(xn_ref[...] * jnp.where(j < nj - 1, 1.0, 0.0))
    w3 = cw_ref.shape[1]
    ext_scr[...] = _dot(jnp.concatenate([hp, h, hn], axis=0), wa_ref[:, 0:w3])
    perm = {}
    for idx, d in enumerate((4, 16)):
        pm = perm_ref[idx]
        perm[d] = jnp.concatenate(
            [_dot(pm, h[t * PERM_TILE:(t + 1) * PERM_TILE]).astype(BF16) for t in range(nblk)], axis=0)
    gate_ref[...] = _dot(h, wgate_ref[...]).astype(BF16)
    q0_ref[...] = _dot(h, wg0_ref[...]).astype(BF16)
    cw = cw_ref[...]
    y = _silu(cw[0:1] * ext_scr[BF16_ROWS - 1:BF16_ROWS - 1 + tm, :]
              + cw[1:2] * ext_scr[BF16_ROWS:BF16_ROWS + tm, :]
              + cw[2:3] * ext_scr[BF16_ROWS + 1:BF16_ROWS + 1 + tm, :])
    q = y[:, 0:WIDTH]
    k = y[:, WIDTH:2 * WIDTH]
    dv_ref[...] = y[:, 2 * WIDTH:3 * WIDTH].astype(BF16)
    hm = hm_ref[...]
    msq_q = _head_meansq(q, hm)
    msq_k = _head_meansq(k, hm)
    z_ref[...] = _dot(h, wa_ref[:, w3:w3 + WIDTH]).astype(BF16)
    ab_ref[...] = _dot(h, wab_ref[...])
    for d, w_ref, ref in ((4, wg1_ref, q1_ref), (16, wg2_ref, q2_ref)):
        run = PERM_TILE // d
        yp = _dot(perm[d], w_ref[...]).astype(BF16)
        for t in range(nblk):
            for r in range(d):
                ref[r, t * run:(t + 1) * run, :] = yp[t * PERM_TILE + r * run:t * PERM_TILE + (r + 1) * run]
    dq_ref[...] = (q * lax.rsqrt(msq_q * HEAD_DIM + EPS)).astype(BF16)
    dk_ref[...] = (k * lax.rsqrt(msq_k * HEAD_DIM + EPS)).astype(BF16)


def _in_proj(x, norm_w, w_a, w_gate, w_ab, w_grp, conv_w, hm, tm=2 * PERM_TILE):
    b, s, dm = x.shape
    w3 = 3 * WIDTH
    tok = lambda i, j: (i, j, 0)
    per = tm // BF16_ROWS
    nrow = s // BF16_ROWS

    def resident(shape):
        return pl.BlockSpec(shape, lambda i, j: (0, 0), pipeline_mode=pl.Buffered(1))

    return pl.pallas_call(
        _in_proj_kernel,
        grid=(b, s // tm),
        in_specs=[pl.BlockSpec((None, tm, dm), tok),
                  pl.BlockSpec((None, BF16_ROWS, dm), lambda i, j: (i, jnp.maximum(j * per - 1, 0), 0)),
                  pl.BlockSpec((None, BF16_ROWS, dm), lambda i, j: (i, jnp.minimum((j + 1) * per, nrow - 1), 0)),
                  resident((1, dm)), resident(w_a.shape), resident(w_gate.shape), resident(w_ab.shape)]
        + [resident((dm, w3))] * 3 + [resident((3, w3)), resident((WIDTH, WIDTH)),
                                      pl.BlockSpec((2, PERM_TILE, PERM_TILE), lambda i, j: (0, 0, 0),
                                                   pipeline_mode=pl.Buffered(1))],
        out_specs=[pl.BlockSpec((None, tm, WIDTH), tok)] * 4 + [
            pl.BlockSpec((None, tm, 2 * dm), tok),
            pl.BlockSpec((None, tm, LANES), tok),
            pl.BlockSpec((None, tm, w3), tok),
            pl.BlockSpec((None, 4, tm // 4, w3), lambda i, j: (i, 0, j, 0)),
            pl.BlockSpec((None, 16, tm // 16, w3), lambda i, j: (i, 0, j, 0)),
        ],
        out_shape=[jax.ShapeDtypeStruct((b, s, WIDTH), BF16)] * 4 + [
            jax.ShapeDtypeStruct((b, s, 2 * dm), BF16),
            jax.ShapeDtypeStruct((b, s, LANES), F32),
            jax.ShapeDtypeStruct((b, s, w3), BF16),
            jax.ShapeDtypeStruct((b, 4, s // 4, w3), BF16),
            jax.ShapeDtypeStruct((b, 16, s // 16, w3), BF16),
        ],
        scratch_shapes=[pltpu.VMEM((tm + 2 * BF16_ROWS, w3), F32)],
        compiler_params=_cparams(("parallel", "parallel")),
        name="in_proj",
    )(x, x, x, norm_w, w_a, w_gate, w_ab, *w_grp, conv_w, hm,
      jnp.stack([_perm_matrix(PERM_TILE, 4), _perm_matrix(PERM_TILE, 16)]))


def _dn_gates_kernel(ab_ref, alog_ref, dtb_ref, tri_ref, place_ref, cp_ref):
    t = tri_ref.shape[1]
    blocks = [slice(r0, r0 + t) for r0 in range(0, ab_ref.shape[0], t)]
    lane = _iota((t, 16), 1)
    gs, betas = [], []
    for rows in blocks:
        ab = ab_ref[rows, :]
        a = ab[:, 0:16] + dtb_ref[...]
        softplus = jnp.maximum(a, 0.0) + jnp.log(1.0 + jnp.exp(-jnp.abs(a)))
        gs.append(_split3(-jnp.exp(alog_ref[...]) * softplus))
        betas.append(_sigmoid(ab[:, 16:32]))
    fwd = [[_dot(tri_ref[0], p) for p in g3] for g3 in gs]
    bwd = [[_dot(tri_ref[1], p) for p in g3] for g3 in gs]
    pieces = [_split3(jnp.where(lane < HEADS, f[0] + f[1] + f[2], r[0] + r[1] + r[2])) + _split3(beta)
              for f, r, beta in zip(fwd, bwd, betas)]
    placed = [[_dot(piece, place_ref[idx]) for idx, piece in enumerate(ps)] for ps in pieces]
    for rows, pl6 in zip(blocks, placed):
        cp_ref[rows, :] = (pl6[0] + pl6[1] + pl6[2] + pl6[3] + pl6[4] + pl6[5]).astype(BF16)


def _dn_gates_constants(tp):
    r = jnp.arange(tp)[:, None]
    c = jnp.arange(tp)[None, :]
    same = (r // CHUNK) == (c // CHUNK)
    tri = jnp.stack([same & (r >= c), same & (r <= c)]).astype(BF16)
    idx = jnp.arange(6)[:, None, None]
    place = (jnp.arange(LANES)[None, None, :] == jnp.arange(16)[None, :, None] + 16 * idx).astype(BF16)
    return tri, place


def _dn_gates(ab, alog, dtb, tp=1024, blk=256):
    b, s, _ = ab.shape
    tp = min(tp, s)
    tri, place = _dn_gates_constants(blk)
    const2 = lambda i, j: (0, 0)
    const3 = lambda i, j: (0, 0, 0)
    return pl.pallas_call(
        _dn_gates_kernel,
        grid=(b, s // tp),
        in_specs=[
            pl.BlockSpec((None, tp, LANES), lambda i, j: (i, j, 0)),
            pl.BlockSpec((1, 16), const2),
            pl.BlockSpec((1, 16), const2),
            pl.BlockSpec((2, blk, blk), const3, pipeline_mode=pl.Buffered(1)),
            pl.BlockSpec((6, 16, LANES), const3, pipeline_mode=pl.Buffered(1)),
        ],
        out_specs=pl.BlockSpec((None, tp, LANES), lambda i, j: (i, j, 0)),
        out_shape=jax.ShapeDtypeStruct((b, s, LANES), BF16),
        compiler_params=_cparams(("parallel", "parallel")),
        name="dn_gates",
    )(ab, alog, dtb, tri, place)


def _block_diag(x):
    blk = _div(_iota(x.shape, 1), HEAD_DIM)
    parts = [jnp.where(blk == a, x, 0.0) for a in range(4)]
    return jnp.concatenate(parts, axis=0).astype(BF16)


def _dn_scan_kernel(q_ref, k_ref, v_ref, cp_ref, e2_ref, o_ref,
                    s_scr, u_scr, l1_scr, l2_scr, al_scr):
    dirn = pl.program_id(1)
    seg = pl.program_id(2)
    nb = q_ref.shape[0]
    nchunk = q_ref.shape[1] // CHUNK
    fwd = dirn == 0

    @pl.when(seg == 0)
    def _():
        s_scr[...] = jnp.zeros_like(s_scr)

    row = _iota((CHUNK, QUAD), 0)
    col = _mod(_iota((CHUNK, QUAD), 1), HEAD_DIM)
    later = jnp.where(fwd, row, col)
    earlier = jnp.where(fwd, col, row)
    incl = later >= earlier
    strict = later > earlier
    diag = row == col
    eye = diag.astype(BF16)
    unroll = min(nchunk, max(1, PHASE_A_PROBLEMS // (2 * nb)))
    assert nchunk % unroll == 0, (nchunk, unroll)

    def phase_a(it, carry):
        probs = []
        for u in range(unroll):
            c = it * unroll + u
            rows = pl.ds(pl.multiple_of(c * CHUNK, CHUNK), CHUNK)
            x_all = _dot(jnp.concatenate([cp_ref[bi, rows, :] for bi in range(nb)], axis=0), e2_ref[...])
            for bi in range(nb):
                x = x_all[bi * CHUNK:(bi + 1) * CHUNK]
                for qd in range(2):
                    lanes = slice(qd * QUAD, (qd + 1) * QUAD)
                    probs.append(dict(
                        c=c, slot=2 * bi + qd, gcol=x[:, qd * QUAD:(qd + 1) * QUAD],
                        beta=x[:, WIDTH + qd * QUAD:WIDTH + (qd + 1) * QUAD],
                        kq=k_ref[bi, rows, lanes], qq=q_ref[bi, rows, lanes], vq=v_ref[bi, rows, lanes]))
        for pr in probs:
            pr["kf"] = pr["kq"].astype(F32)
            pr["z"] = _dot_nt(jnp.concatenate([pr["kq"], pr["qq"], eye], axis=0),
                              _block_diag(pr["kf"]))
        for pr in probs:
            gcol, beta, z = pr["gcol"], pr["beta"], pr["z"]
            grow = jnp.sum(jnp.where(diag, gcol, 0.0), axis=0, keepdims=True)
            glast = jnp.where(fwd, gcol[CHUNK - 1:CHUNK, :], gcol[0:1, :])
            dm = jnp.exp(jnp.where(incl, gcol - grow, NEG_INF))
            eg = jnp.exp(gcol)
            a = jnp.where(strict, beta * z[0:CHUNK] * dm, 0.0)
            qkm = jnp.where(incl, z[CHUNK:2 * CHUNK] * dm, 0.0) * 0.125
            kdt = z[2 * CHUNK:3 * CHUNK] * jnp.exp(glast - grow)
            pr["vb"] = pr["vq"].astype(F32) * beta
            pr["kbg"] = pr["kf"] * (beta * eg)
            qdec = pr["qq"].astype(F32) * (eg * 0.125)
            c, slot = pr["c"], pr["slot"]
            l2_scr[slot, c] = jnp.concatenate([qkm, kdt], axis=0).astype(BF16)
            l1_scr[slot, c, CHUNK:2 * CHUNK, :] = qdec.astype(BF16)
            al_scr[slot, c] = jnp.broadcast_to(jnp.exp(glast), (8, QUAD))
            pr["a"] = a
        for pr in probs:
            pr["p"] = _dot(pr["a"].astype(BF16), _block_diag(pr["a"]))
            pr["n"] = -pr["a"]
        for _ in range(4):
            for pr in probs:
                n, p = pr["n"], pr["p"]
                zz = _dot(jnp.concatenate([n, p], axis=0).astype(BF16), _block_diag(p))
                pr["n"] = n + p + zz[0:CHUNK]
                pr["p"] = zz[CHUNK:2 * CHUNK]
        for pr in probs:
            n, p = pr["n"], pr["p"]
            pr["n"] = n + p + _dot(n.astype(BF16), _block_diag(p))
        for pr in probs:
            vb, kbg, c, slot = pr["vb"], pr["kbg"], pr["c"], pr["slot"]
            rhs = jnp.concatenate([_block_diag(vb), _block_diag(kbg)], axis=1)
            uw = _dot(pr["n"].astype(BF16), rhs)
            u_scr[slot, c] = vb + uw[:, 0:QUAD]
            l1_scr[slot, c, 0:CHUNK, :] = (kbg + uw[:, QUAD:2 * QUAD]).astype(BF16)
        return carry

    lax.fori_loop(0, nchunk // unroll, phase_a, 0)

    nslot = 2 * nb

    def phase_b(i, states):
        c = jnp.where(fwd, i, nchunk - 1 - i)
        r0 = pl.multiple_of(c * CHUNK, CHUNK)
        z1 = [_dot(l1_scr[sl, c], _block_diag(states[sl])) for sl in range(nslot)]
        vn = [u_scr[sl, c] - z1[sl][0:CHUNK] for sl in range(nslot)]
        z2 = [_dot(l2_scr[sl, c], _block_diag(vn[sl])) for sl in range(nslot)]
        new = []
        for sl in range(nslot):
            o = z1[sl][CHUNK:2 * CHUNK] + z2[sl][0:CHUNK]
            o_ref[sl // 2, pl.ds(r0, CHUNK), (sl % 2) * QUAD:(sl % 2 + 1) * QUAD] = o.astype(o_ref.dtype)
            new.append(al_scr[sl, c][0:1, :] * states[sl] + z2[sl][CHUNK:2 * CHUNK])
        return tuple(new)

    final = lax.fori_loop(0, nchunk, phase_b, tuple(s_scr[sl] for sl in range(nslot)))
    for sl in range(nslot):
        s_scr[sl] = final[sl]


def _dn_scan(q, k, v, cp, e2, nb=8, seg=256):
    b, s, _ = q.shape
    while b % nb:
        nb //= 2
    seg = min(seg, s)
    nseg = s // seg
    nchunk = seg // CHUNK

    def tok(i, d, j):
        return (i, jnp.where(d == 0, j, nseg - 1 - j), 0)

    return pl.pallas_call(
        _dn_scan_kernel,
        grid=(b // nb, 2, nseg),
        in_specs=[pl.BlockSpec((nb, seg, WIDTH), tok)] * 3
        + [pl.BlockSpec((nb, seg, LANES), tok),
           pl.BlockSpec((None, LANES, 2 * WIDTH), lambda i, d, j: (d, 0, 0))],
        out_specs=pl.BlockSpec((None, nb, seg, WIDTH),
                               lambda i, d, j: (d, i, jnp.where(d == 0, j, nseg - 1 - j), 0)),
        out_shape=jax.ShapeDtypeStruct((2, b, s, WIDTH), BF16),
        scratch_shapes=[
            pltpu.VMEM((2 * nb, CHUNK, QUAD), F32),
            pltpu.VMEM((2 * nb, nchunk, CHUNK, QUAD), F32),
            pltpu.VMEM((2 * nb, nchunk, 2 * CHUNK, QUAD), BF16),
            pltpu.VMEM((2 * nb, nchunk, 2 * CHUNK, QUAD), BF16),
            pltpu.VMEM((2 * nb, nchunk, 8, QUAD), F32),
        ],
        compiler_params=_cparams(("parallel", "arbitrary", "arbitrary")),
        name="dn_scan",
    )(q, k, v, cp, e2)


def _dn_expand_matrix():
    r = jnp.arange(LANES)[:, None]
    c = jnp.arange(2 * WIDTH)[None, :]
    piece, lane = r // 16, r % 16
    out = []
    for d in range(2):
        head = lane - 8 * d
        ok = (piece < 6) & (head >= 0) & (head < HEADS)
        ok = ok & ((c // WIDTH) == (piece // 3)) & (((c % WIDTH) // HEAD_DIM) == head)
        out.append(ok)
    return jnp.stack(out).astype(BF16)


QBLK = 2 * RADIUS
ATTN_ROWS = 1024


def _attn_kernel(q_ref, k_ref, kp_ref, kn_ref, v_ref, vp_ref, vn_ref, qw_ref, kw_ref, hm_ref, bias_ref,
                 o_ref, lse_ref, *, sub):
    for si in range(q_ref.shape[0]):
        _attn_tile(*(r.at[si] for r in (q_ref, k_ref, kp_ref, kn_ref, v_ref, vp_ref, vn_ref)),
                   qw_ref, kw_ref, hm_ref, bias_ref, o_ref.at[si], lse_ref.at[si], sub=sub)


def _attn_tile(q_ref, k_ref, kp_ref, kn_ref, v_ref, vp_ref, vn_ref, qw_ref, kw_ref, hm_ref, bias_ref,
               o_ref, lse_ref, *, sub):
    t = pl.program_id(1)
    tq = q_ref.shape[0]
    nqb = tq // QBLK
    low = _mod(_iota((1, WIDTH), 1), LANES) < HEAD_DIM
    hm = hm_ref[...]
    q = q_ref[...].astype(F32)
    qn = q * lax.rsqrt(_dot((q * q).astype(BF16), hm) + EPS) * (qw_ref[...] * (HEAD_DIM ** -0.5 * LOG2E))
    qn = qn.astype(BF16)
    zero = jnp.zeros((), BF16)
    one = jnp.ones((), BF16)
    qsel = (jnp.where(low, qn, zero), jnp.where(low, zero, qn))
    kx = jnp.concatenate([kp_ref[...], k_ref[...], kn_ref[...]], axis=0).astype(F32)
    kn = (kx * lax.rsqrt(_dot((kx * kx).astype(BF16), hm) + EPS) * kw_ref[...]).astype(BF16)
    vx = jnp.concatenate([vp_ref[...], v_ref[...], vn_ref[...]], axis=0)
    vsel = (jnp.where(low, vx, one), jnp.where(low, one, vx))
    c = _iota((QBLK, 2 * QBLK), 1)
    lane = _iota((1, LANES), 1)
    low_pair = lane < HEAD_DIM
    pairs = [slice((h // 2) * LANES, (h // 2 + 1) * LANES) for h in range(HEADS)]
    for qb in range(nqb):
        rows = slice(qb * QBLK, (qb + 1) * QBLK)
        krows = slice(qb * QBLK, qb * QBLK + 2 * QBLK)
        scores = [_dot_nt(qsel[h % 2][rows, pairs[h]], kn[krows, pairs[h]]) for h in range(HEADS)]
        edge = None
        if qb == 0 or qb == nqb - 1:
            kj = t * tq + (qb * QBLK - RADIUS) + c
            edge = jnp.where((kj >= 0) & (kj < sub), 0.0, NEG_INF)
        probs, maxes = [], []
        for h in range(HEADS):
            sc = scores[h] + bias_ref[h]
            if edge is not None:
                sc = sc + edge
            m = jnp.max(sc, axis=-1, keepdims=True)
            probs.append(jnp.exp2(sc - m).astype(BF16))
            maxes.append(m)
        outs = [_dot(probs[h], vsel[h % 2][krows, pairs[h]]) for h in range(HEADS)]
        lse_tile = jnp.zeros((QBLK, LANES), F32)
        for h0 in range(0, HEADS, 2):
            even, odd = outs[h0], outs[h0 + 1]
            num = jnp.where(low_pair, even, odd)
            den = pltpu.roll(jnp.where(low_pair, odd, even), HEAD_DIM, 1)
            o_ref[rows, pairs[h0]] = (num / den).astype(o_ref.dtype)
            lse_pair = jnp.where(low_pair, maxes[h0], maxes[h0 + 1]) * LN2 + jnp.log(den)
            keep = (lane == _lse_lane(h0)) | (lane == _lse_lane(h0 + 1))
            lse_tile = lse_tile + jnp.where(keep, lse_pair, 0.0)
        lse_ref[rows, :] = lse_tile


def _alibi_bias(dil, group):
    a = jnp.arange(QBLK)[:, None]
    c = jnp.arange(2 * QBLK)[None, :]
    rel = jnp.abs(a + RADIUS - c)
    slopes = 2.0 ** (-ALIBI_MAX * (group * HEADS + jnp.arange(1, HEADS + 1, dtype=F32)) / N_ATT_HEADS)
    bias = -(slopes * LOG2E)[:, None, None] * (rel * dil).astype(F32)[None]
    return jnp.where((rel <= RADIUS)[None], bias, NEG_INF)


def _attn(qkv, qw, kw, hm, *, dil, group):
    nseq, sub, _ = qkv.shape
    tq = min(ATTN_ROWS, sub)
    ns = max(1, ATTN_ROWS // sub)
    while nseq % ns:
        ns //= 2
    nt = sub // tq
    per = tq // RADIUS
    nblk = sub // RADIUS

    def main(lb):
        return pl.BlockSpec((ns, tq, WIDTH), lambda i, j: (i, j, lb))

    def prev(lb):
        return pl.BlockSpec((ns, RADIUS, WIDTH), lambda i, j: (i, jnp.maximum(j * per - 1, 0), lb))

    def nxt(lb):
        return pl.BlockSpec((ns, RADIUS, WIDTH),
                            lambda i, j: (i, jnp.minimum((j + 1) * per, nblk - 1), lb))

    wspec = pl.BlockSpec((1, WIDTH), lambda i, j: (0, 0))
    hmspec = pl.BlockSpec((WIDTH, WIDTH), lambda i, j: (0, 0), pipeline_mode=pl.Buffered(1))
    bspec = pl.BlockSpec((HEADS, QBLK, 2 * QBLK), lambda i, j: (0, 0, 0), pipeline_mode=pl.Buffered(1))
    return pl.pallas_call(
        functools.partial(_attn_kernel, sub=sub),
        grid=(nseq // ns, nt),
        in_specs=[main(0), main(1), prev(1), nxt(1), main(2), prev(2), nxt(2), wspec, wspec, hmspec, bspec],
        out_specs=[pl.BlockSpec((ns, tq, WIDTH), lambda i, j: (i, j, 0)),
                   pl.BlockSpec((ns, tq, LANES), lambda i, j: (i, j, 0))],
        out_shape=[jax.ShapeDtypeStruct((nseq, sub, WIDTH), BF16),
                   jax.ShapeDtypeStruct((nseq, sub, LANES), F32)],
        compiler_params=_cparams(("parallel", "parallel")),
        name=f"attn_g{group}",
    )(qkv, qkv, qkv, qkv, qkv, qkv, qkv, qw, kw, hm, _alibi_bias(dil, group))


def _merge_kernel(of_ref, ob_ref, z_ref, o0_ref, o1_ref, o2_ref, l0_ref, l1_ref, l2_ref,
                  gate_ref, x_ref, wdn_ref, wat_ref, wo_ref, dnw_ref, n2w_ref, hm_ref, pinv_ref, expand_ref,
                  x1_ref, h2_ref):
    tm = x_ref.shape[0]
    dm = x_ref.shape[1]
    sub = pinv_ref.shape[1]
    expand = expand_ref[...]
    blocks = [dict(rows=slice(t * sub, (t + 1) * sub), t=t) for t in range(tm // sub)]
    for bk in blocks:
        rows, t = bk["rows"], bk["t"]
        oa = of_ref[rows, :].astype(F32) + ob_ref[rows, :].astype(F32)
        bk["oa"] = oa
        bk["msq"] = _head_meansq(oa, hm_ref[...])
        outs = [o0_ref[rows, :].astype(F32)]
        lses = [l0_ref[rows, :]]
        for idx, (d, oref, lref) in enumerate(((4, o1_ref, l1_ref), (16, o2_ref, l2_ref))):
            pinv = pinv_ref[idx]
            run = sub // d
            part = slice(t * run, (t + 1) * run)
            outs.append(_dot(pinv, jnp.concatenate([oref[r, part, :] for r in range(d)], axis=0)))
            lflat = jnp.concatenate([lref[r, part, :] for r in range(d)], axis=0)
            lp = _dot(pinv, jnp.concatenate(_split3(lflat), axis=1))
            lses.append(lp[:, 0:LANES] + lp[:, LANES:2 * LANES] + lp[:, 2 * LANES:3 * LANES])
        bk["outs"], bk["lses"] = outs, lses
    for bk in blocks:
        rows = bk["rows"]
        z = z_ref[rows, :].astype(F32)
        gated = bk["oa"] * lax.rsqrt(bk["msq"] + EPS) * dnw_ref[...] * _silu(z)
        bk["ya"] = _dot(gated.astype(BF16), wdn_ref[...])
        lses = bk["lses"]
        m = jnp.maximum(jnp.maximum(lses[0], lses[1]), lses[2])
        es = [jnp.exp(l - m) for l in lses]
        den = es[0] + es[1] + es[2]
        ob = jnp.zeros((sub, WIDTH), F32)
        for e, o in zip(es, bk["outs"]):
            hi, mid, _ = _split3(e / den)
            ob = ob + _dot(jnp.concatenate([hi, mid], axis=1), expand) * o
        bk["ob"] = ob
    for bk in blocks:
        bk["yb"] = _dot(bk["ob"].astype(BF16), wat_ref[...])
    for bk in blocks:
        rows = bk["rows"]
        g = gate_ref[rows, :].astype(F32)
        mixed = _sigmoid(g[:, 0:dm]) * bk["ya"] + _sigmoid(g[:, dm:2 * dm]) * bk["yb"]
        x1 = x_ref[rows, :] + _dot(mixed.astype(BF16), wo_ref[...])
        x1_ref[rows, :] = x1
        ms = jnp.mean(x1 * x1, axis=-1, keepdims=True)
        h2_ref[rows, :] = (x1 * lax.rsqrt(ms + EPS) * n2w_ref[...]).astype(BF16)


def _merge(o_dn, z, attn_o, attn_l, gates, x, wdn, wat, wo, dnw, n2w, hm, tm=2 * PERM_TILE):
    b, s, dm = x.shape
    tok = lambda i, j: (i, j, 0)
    const = lambda i, j: (0, 0)
    pinv = jnp.stack([_perm_matrix(PERM_TILE, 4, inverse=True), _perm_matrix(PERM_TILE, 16, inverse=True)])
    in_specs = [
        pl.BlockSpec((None, None, tm, WIDTH), lambda i, j: (0, i, j, 0)),
        pl.BlockSpec((None, None, tm, WIDTH), lambda i, j: (1, i, j, 0)),
        pl.BlockSpec((None, tm, WIDTH), tok),
        pl.BlockSpec((None, tm, WIDTH), tok),
        pl.BlockSpec((None, 4, tm // 4, WIDTH), lambda i, j: (i, 0, j, 0)),
        pl.BlockSpec((None, 16, tm // 16, WIDTH), lambda i, j: (i, 0, j, 0)),
        pl.BlockSpec((None, tm, LANES), tok),
        pl.BlockSpec((None, 4, tm // 4, LANES), lambda i, j: (i, 0, j, 0)),
        pl.BlockSpec((None, 16, tm // 16, LANES), lambda i, j: (i, 0, j, 0)),
        pl.BlockSpec((None, tm, 2 * dm), tok),
        pl.BlockSpec((None, tm, dm), tok),
        pl.BlockSpec((WIDTH, dm), const),
        pl.BlockSpec((WIDTH, dm), const),
        pl.BlockSpec((dm, dm), const),
        pl.BlockSpec((1, WIDTH), const),
        pl.BlockSpec((1, dm), const),
        pl.BlockSpec((WIDTH, WIDTH), const, pipeline_mode=pl.Buffered(1)),
        pl.BlockSpec((2, PERM_TILE, PERM_TILE), lambda i, j: (0, 0, 0), pipeline_mode=pl.Buffered(1)),
        pl.BlockSpec((2 * LANES, WIDTH), const, pipeline_mode=pl.Buffered(1)),
    ]
    expand = _head_expand()
    return pl.pallas_call(
        _merge_kernel,
        grid=(b, s // tm),
        in_specs=in_specs,
        out_specs=[pl.BlockSpec((None, tm, dm), tok), pl.BlockSpec((None, tm, dm), tok)],
        out_shape=[jax.ShapeDtypeStruct((b, s, dm), F32), jax.ShapeDtypeStruct((b, s, dm), BF16)],
        compiler_params=_cparams(("parallel", "parallel")),
        name="merge",
    )(o_dn, o_dn, z, attn_o[0], attn_o[1], attn_o[2], attn_l[0], attn_l[1], attn_l[2],
      gates, x, wdn, wat, wo, dnw, n2w, hm, pinv, jnp.concatenate([expand, expand], axis=0))


def _ffn_kernel(h_ref, hp_ref, hn_ref, wup_ref, cw_ref, wd_ref, x1_ref, o_ref,
                ug_ref, uu_ref, *, tiles_per_seq, chunks):
    i = pl.program_id(0)
    tm = h_ref.shape[0]
    dff = wd_ref.shape[0]
    pos = _mod(i, tiles_per_seq)
    hp = hp_ref[...] * jnp.where(pos > 0, 1.0, 0.0).astype(BF16)
    hn = hn_ref[...] * jnp.where(pos < tiles_per_seq - 1, 1.0, 0.0).astype(BF16)
    lhs = jnp.concatenate([hp, h_ref[...], hn], axis=0)

    def conv(c0, c1, ue_ref):
        w = c1 - c0
        cw = cw_ref[:, c0:c1]
        return (cw[0:1] * ue_ref[BF16_ROWS - 1:BF16_ROWS - 1 + tm, 0:w]
                + cw[1:2] * ue_ref[BF16_ROWS:BF16_ROWS + tm, 0:w]
                + cw[2:3] * ue_ref[BF16_ROWS + 1:BF16_ROWS + 1 + tm, 0:w])

    for idx, (c0, c1) in enumerate(chunks):
        ug_ref[idx, :, 0:c1 - c0] = _dot(lhs, wup_ref[:, c0:c1])
        uu_ref[idx, :, 0:c1 - c0] = _dot(lhs, wup_ref[:, dff + c0:dff + c1])
    out = x1_ref[...]
    for idx, (c0, c1) in enumerate(chunks):
        act = _silu(conv(c0, c1, ug_ref.at[idx])) * conv(dff + c0, dff + c1, uu_ref.at[idx])
        out = out + _dot(act.astype(BF16), wd_ref[c0:c1, :])
    o_ref[...] = out


MXU_TILE = 256
FFN_CHUNK_TILES = 3


def _ffn(h2, x1, w_up, conv_w, w_down, seq, tm=512):
    n, dm = h2.shape
    dff = w_down.shape[0]
    split = FFN_CHUNK_TILES * MXU_TILE
    chunks = tuple((c0, min(c0 + split, dff)) for c0 in range(0, dff, split))
    per = tm // BF16_ROWS
    nblk = n // BF16_ROWS
    const = lambda i: (0, 0)

    def resident(shape):
        return pl.BlockSpec(shape, const, pipeline_mode=pl.Buffered(1))

    return pl.pallas_call(
        functools.partial(_ffn_kernel, tiles_per_seq=seq // tm, chunks=chunks),
        grid=(n // tm,),
        in_specs=[
            pl.BlockSpec((tm, dm), lambda i: (i, 0)),
            pl.BlockSpec((BF16_ROWS, dm), lambda i: (jnp.maximum(i * per - 1, 0), 0)),
            pl.BlockSpec((BF16_ROWS, dm), lambda i: (jnp.minimum((i + 1) * per, nblk - 1), 0)),
            resident(w_up.shape),
            resident(conv_w.shape),
            resident(w_down.shape),
            pl.BlockSpec((tm, dm), lambda i: (i, 0)),
        ],
        out_specs=pl.BlockSpec((tm, dm), lambda i: (i, 0)),
        out_shape=jax.ShapeDtypeStruct((n, dm), F32),
        scratch_shapes=[pltpu.VMEM((len(chunks), tm + 2 * BF16_ROWS, split), F32),
                        pltpu.VMEM((len(chunks), tm + 2 * BF16_ROWS, split), F32)],
        compiler_params=_cparams(("parallel",)),
        name="ffn",
    )(h2, h2, h2, w_up, conv_w, w_down, x1)


def _layer(x, norm1_w, w_in, dn_conv_w, dn_a_log, dn_dt_bias, dn_out_norm_w, attn_q_norm_w,
           attn_k_norm_w, w_dn_out, w_attn_out, w_o, norm2_w, w_ffn_up, ffn_conv_w, w_ffn_down):
    b, s, dm = x.shape
    n = b * s
    w3 = 3 * WIDTH
    assert s % (max(DILATIONS) * QBLK) == 0 and s % (2 * PERM_TILE) == 0, s
    assert w_in.shape == (dm, 4 * w3 + WIDTH + 32 + 2 * dm), w_in.shape
    c_z = w3
    c_ab = c_z + WIDTH
    c_q = c_ab + 32
    c_k = c_q + w3
    c_v = c_k + w3
    c_gate = c_v + w3
    wb = w_in.astype(BF16)
    w_a = wb[:, 0:c_ab]
    w_ab = jnp.pad(wb[:, c_ab:c_q], ((0, 0), (0, LANES - 32)))
    w_grp = [jnp.concatenate([wb[:, c0 + g * WIDTH:c0 + (g + 1) * WIDTH] for c0 in (c_q, c_k, c_v)], axis=1)
             for g in range(3)]
    w_gate = wb[:, c_gate:c_gate + 2 * dm]

    hm = _head_mean_matrix()

    q, k, v, z, gates, ab, qkv0, qkv1, qkv2 = _in_proj(x, norm1_w.reshape(1, dm), w_a, w_gate, w_ab, w_grp,
                                                       dn_conv_w, hm)

    cp = _dn_gates(ab, dn_a_log.reshape(1, 16), dn_dt_bias.reshape(1, 16))
    o_dn = _dn_scan(q, k, v, cp, _dn_expand_matrix())

    qw = jnp.tile(attn_q_norm_w, HEADS).reshape(1, WIDTH)
    kw = jnp.tile(attn_k_norm_w, HEADS).reshape(1, WIDTH)
    attn_o, attn_l = [], []
    for g, (d, qkv) in enumerate(zip(DILATIONS, (qkv0, qkv1, qkv2))):
        o, l = _attn(qkv.reshape(b * d, s // d, w3), qw, kw, hm, dil=d, group=g)
        shape = (b, s) if d == 1 else (b, d, s // d)
        attn_o.append(o.reshape(shape + (WIDTH,)))
        attn_l.append(l.reshape(shape + (LANES,)))

    x1, h2 = _merge(o_dn, z, attn_o, attn_l, gates, x, w_dn_out.astype(BF16), w_attn_out.astype(BF16),
                    w_o.astype(BF16), jnp.tile(dn_out_norm_w, HEADS).reshape(1, WIDTH),
                    norm2_w.reshape(1, dm), hm)
    out = _ffn(h2.reshape(n, dm), x1.reshape(n, dm), w_ffn_up.astype(BF16), ffn_conv_w,
               w_ffn_down.astype(BF16), s)
    return out.reshape(b, s, dm)


def kernel(x, norm1_w, w_in, dn_conv_w, dn_a_log, dn_dt_bias, dn_out_norm_w, attn_q_norm_w, attn_k_norm_w, w_dn_out, w_attn_out, w_o, norm2_w, w_ffn_up, ffn_conv_w, w_ffn_down):
    for layer in range(norm1_w.shape[0]):
        x = _layer(x, norm1_w[layer], w_in[layer], dn_conv_w[layer], dn_a_log[layer], dn_dt_bias[layer],
                   dn_out_norm_w[layer], attn_q_norm_w[layer], attn_k_norm_w[layer], w_dn_out[layer],
                   w_attn_out[layer], w_o[layer], norm2_w[layer], w_ffn_up[layer], ffn_conv_w[layer],
                   w_ffn_down[layer])
    return x
```

```python
import functools

import jax
import jax.numpy as jnp
from jax import lax
from jax.experimental import pallas as pl
from jax.experimental.pallas import tpu as pltpu

F32 = jnp.float32
BF16 = jnp.bfloat16

EPS = 1e-6
NEG_INF = -1e30
LOG2E = 1.4426950408889634
LN2 = 0.6931471805599453

HEAD_DIM = 64
HEADS = 8
WIDTH = HEADS * HEAD_DIM
CHUNK = 64
RADIUS = 64
DILATIONS = (1, 4, 16)
ALIBI_MAX = 8.0
N_ATT_HEADS = 24
QUAD = 4 * HEAD_DIM
PHASE_A_PROBLEMS = 32
BF16_ROWS = 16
PERM_TILE = 256
LANES = 128
V7X_VMEM_LIMIT = 56 * 1024 * 1024


def _cparams(sem):
    return pltpu.CompilerParams(dimension_semantics=sem, vmem_limit_bytes=V7X_VMEM_LIMIT)


def _dot(a, b):
    return jnp.dot(a, b, preferred_element_type=F32)


def _dot_nt(a, b):
    return lax.dot_general(a, b, (((1,), (1,)), ((), ())), preferred_element_type=F32)


def _split3(x):
    hi = x.astype(BF16)
    r1 = x - hi.astype(F32)
    mid = r1.astype(BF16)
    lo = (r1 - mid.astype(F32)).astype(BF16)
    return hi, mid, lo


def _dot3(a, x):
    hi, mid, lo = _split3(x)
    return _dot(a, hi) + _dot(a, mid) + _dot(a, lo)


def _log2(n):
    assert n > 0 and n & (n - 1) == 0, n
    return n.bit_length() - 1


def _div(x, n):
    return lax.shift_right_logical(x, _log2(n))


def _mod(x, n):
    assert n & (n - 1) == 0, n
    return x & (n - 1)


def _iota(shape, dim):
    return lax.broadcasted_iota(jnp.int32, shape, dim)


def _sigmoid(x):
    return 0.5 * jnp.tanh(0.5 * x) + 0.5


def _silu(x):
    h = 0.5 * x
    return h + h * jnp.tanh(h)


def _head_mean_matrix():
    r = jnp.arange(WIDTH)[:, None] // HEAD_DIM
    c = jnp.arange(WIDTH)[None, :] // HEAD_DIM
    return jnp.where(r == c, 1.0 / HEAD_DIM, 0.0).astype(BF16)


def _head_meansq(x, hm):
    return _dot((x * x).astype(BF16), hm)


def _lse_lane(h):
    return h if h % 2 == 0 else HEAD_DIM + h


def _head_expand():
    r = _iota((LANES, WIDTH), 0)
    head = _div(_iota((LANES, WIDTH), 1), HEAD_DIM)
    lane = jnp.where(_mod(head, 2) == 0, head, head + HEAD_DIM)
    return (r == lane).astype(BF16)


def _perm_matrix(tm, d, inverse=False):
    run = tm // d
    row = _iota((tm, tm), 0)
    col = _iota((tm, tm), 1)
    if inverse:
        src = _mod(row, d) * run + _div(row, d)
    else:
        src = _mod(row, run) * d + _div(row, run)
    return (col == src).astype(BF16)


def _in_proj_kernel(x_ref, xp_ref, xn_ref, nw_ref, wa_ref, wgate_ref, wg0_ref, wg1_ref, wg2_ref,
                    cw_ref, hm_ref, perm_ref,
                    dq_ref, dk_ref, dv_ref, z_ref, gate_ref, ab_ref, q0_ref, q1_ref, q2_ref, ext_scr):
    j = pl.program_id(1)
    nj = pl.num_programs(1)

    def norm(v):
        ms = jnp.mean(v * v, axis=-1, keepdims=True)
        return (v * lax.rsqrt(ms + EPS) * nw_ref[...]).astype(BF16)

    x = x_ref[...]
    h = norm(x)
    tm = x.shape[0]
    nblk = tm // PERM_TILE
    hp = norm(xp_ref[...] * jnp.where(j > 0, 1.0, 0.0))
    hn = norm(xn_ref[...] * jnp.where(j < nj - 1, 1.0, 0.0))
    w3 = cw_ref.shape[1]
    ext_scr[...] = _dot(jnp.concatenate([hp, h, hn], axis=0), wa_ref[:, 0:w3])
    perm = {}
    for idx, d in enumerate((4, 16)):
        pm = perm_ref[idx]
        perm[d] = jnp.concatenate(
            [_dot(pm, h[t * PERM_TILE:(t + 1) * PERM_TILE]).astype(BF16) for t in range(nblk)], axis=0)
    gate_ref[...] = _dot(h, wgate_ref[...]).astype(BF16)
    q0_ref[...] = _dot(h, wg0_ref[...]).astype(BF16)
    cw = cw_ref[...]
    y = _silu(cw[0:1] * ext_scr[BF16_ROWS - 1:BF16_ROWS - 1 + tm, :]
              + cw[1:2] * ext_scr[BF16_ROWS:BF16_ROWS + tm, :]
              + cw[2:3] * ext_scr[BF16_ROWS + 1:BF16_ROWS + 1 + tm, :])
    q = y[:, 0:WIDTH]
    k = y[:, WIDTH:2 * WIDTH]
    dv_ref[...] = y[:, 2 * WIDTH:3 * WIDTH].astype(BF16)
    hm = hm_ref[...]
    msq_q = _head_meansq(q, hm)
    msq_k = _head_meansq(k, hm)
    zab = _dot(h, wa_ref[:, w3:w3 + WIDTH + LANES])
    z_ref[...] = zab[:, 0:WIDTH].astype(BF16)
    ab_ref[...] = zab[:, WIDTH:WIDTH + LANES]
    for d, w_ref, ref in ((4, wg1_ref, q1_ref), (16, wg2_ref, q2_ref)):
        run = PERM_TILE // d
        yp = _dot(perm[d], w_ref[...]).astype(BF16)
        for t in range(nblk):
            for r in range(d):
                ref[r, t * run:(t + 1) * run, :] = yp[t * PERM_TILE + r * run:t * PERM_TILE + (r + 1) * run]
    dq_ref[...] = (q * lax.rsqrt(msq_q * HEAD_DIM + EPS)).astype(BF16)
    dk_ref[...] = (k * lax.rsqrt(msq_k * HEAD_DIM + EPS)).astype(BF16)


def _in_proj(x, norm_w, w_a, w_gate, w_grp, conv_w, hm, tm=2 * PERM_TILE):
    b, s, dm = x.shape
    w3 = 3 * WIDTH
    tok = lambda i, j: (i, j, 0)
    per = tm // BF16_ROWS
    nrow = s // BF16_ROWS

    def resident(shape):
        return pl.BlockSpec(shape, lambda i, j: (0, 0), pipeline_mode=pl.Buffered(1))

    return pl.pallas_call(
        _in_proj_kernel,
        grid=(b, s // tm),
        in_specs=[pl.BlockSpec((None, tm, dm), tok),
                  pl.BlockSpec((None, BF16_ROWS, dm), lambda i, j: (i, jnp.maximum(j * per - 1, 0), 0)),
                  pl.BlockSpec((None, BF16_ROWS, dm), lambda i, j: (i, jnp.minimum((j + 1) * per, nrow - 1), 0)),
                  resident((1, dm)), resident(w_a.shape), resident(w_gate.shape)]
        + [resident((dm, w3))] * 3 + [resident((3, w3)), resident((WIDTH, WIDTH)),
                                      pl.BlockSpec((2, PERM_TILE, PERM_TILE), lambda i, j: (0, 0, 0),
                                                   pipeline_mode=pl.Buffered(1))],
        out_specs=[pl.BlockSpec((None, tm, WIDTH), tok)] * 4 + [
            pl.BlockSpec((None, tm, 2 * dm), tok),
            pl.BlockSpec((None, tm, LANES), tok),
            pl.BlockSpec((None, tm, w3), tok),
            pl.BlockSpec((None, 4, tm // 4, w3), lambda i, j: (i, 0, j, 0)),
            pl.BlockSpec((None, 16, tm // 16, w3), lambda i, j: (i, 0, j, 0)),
        ],
        out_shape=[jax.ShapeDtypeStruct((b, s, WIDTH), BF16)] * 4 + [
            jax.ShapeDtypeStruct((b, s, 2 * dm), BF16),
            jax.ShapeDtypeStruct((b, s, LANES), F32),
            jax.ShapeDtypeStruct((b, s, w3), BF16),
            jax.ShapeDtypeStruct((b, 4, s // 4, w3), BF16),
            jax.ShapeDtypeStruct((b, 16, s // 16, w3), BF16),
        ],
        scratch_shapes=[pltpu.VMEM((tm + 2 * BF16_ROWS, w3), F32)],
        compiler_params=_cparams(("parallel", "parallel")),
        name="in_proj",
    )(x, x, x, norm_w, w_a, w_gate, *w_grp, conv_w, hm,
      jnp.stack([_perm_matrix(PERM_TILE, 4), _perm_matrix(PERM_TILE, 16)]))


def _dn_gates_kernel(ab_ref, alog_ref, dtb_ref, tri_ref, place_ref, cp_ref):
    t = tri_ref.shape[1]
    blocks = [slice(r0, r0 + t) for r0 in range(0, ab_ref.shape[0], t)]
    lane = _iota((t, 16), 1)
    gs, betas = [], []
    for rows in blocks:
        ab = ab_ref[rows, :]
        a = ab[:, 0:16] + dtb_ref[...]
        softplus = jnp.maximum(a, 0.0) + jnp.log(1.0 + jnp.exp(-jnp.abs(a)))
        gs.append(_split3(-jnp.exp(alog_ref[...]) * softplus))
        betas.append(_sigmoid(ab[:, 16:32]))
    fwd = [[_dot(tri_ref[0], p) for p in g3] for g3 in gs]
    bwd = [[_dot(tri_ref[1], p) for p in g3] for g3 in gs]
    pieces = [_split3(jnp.where(lane < HEADS, f[0] + f[1] + f[2], r[0] + r[1] + r[2])) + _split3(beta)
              for f, r, beta in zip(fwd, bwd, betas)]
    placed = [[_dot(piece, place_ref[idx]) for idx, piece in enumerate(ps)] for ps in pieces]
    for rows, pl6 in zip(blocks, placed):
        cp_ref[rows, :] = (pl6[0] + pl6[1] + pl6[2] + pl6[3] + pl6[4] + pl6[5]).astype(BF16)


def _dn_gates_constants(tp):
    r = jnp.arange(tp)[:, None]
    c = jnp.arange(tp)[None, :]
    same = (r // CHUNK) == (c // CHUNK)
    tri = jnp.stack([same & (r >= c), same & (r <= c)]).astype(BF16)
    idx = jnp.arange(6)[:, None, None]
    place = (jnp.arange(LANES)[None, None, :] == jnp.arange(16)[None, :, None] + 16 * idx).astype(BF16)
    return tri, place


def _dn_gates(ab, alog, dtb, tp=1024, blk=256):
    b, s, _ = ab.shape
    tp = min(tp, s)
    tri, place = _dn_gates_constants(blk)
    const2 = lambda i, j: (0, 0)
    const3 = lambda i, j: (0, 0, 0)
    return pl.pallas_call(
        _dn_gates_kernel,
        grid=(b, s // tp),
        in_specs=[
            pl.BlockSpec((None, tp, LANES), lambda i, j: (i, j, 0)),
            pl.BlockSpec((1, 16), const2),
            pl.BlockSpec((1, 16), const2),
            pl.BlockSpec((2, blk, blk), const3, pipeline_mode=pl.Buffered(1)),
            pl.BlockSpec((6, 16, LANES), const3, pipeline_mode=pl.Buffered(1)),
        ],
        out_specs=pl.BlockSpec((None, tp, LANES), lambda i, j: (i, j, 0)),
        out_shape=jax.ShapeDtypeStruct((b, s, LANES), BF16),
        compiler_params=_cparams(("parallel", "parallel")),
        name="dn_gates",
    )(ab, alog, dtb, tri, place)


def _block_diag(x):
    blk = _div(_iota(x.shape, 1), HEAD_DIM)
    parts = [jnp.where(blk == a, x, 0.0) for a in range(4)]
    return jnp.concatenate(parts, axis=0).astype(BF16)


def _dn_scan_kernel(q_ref, k_ref, v_ref, cp_ref, e2_ref, o_ref,
                    s_scr, u_scr, l1_scr, l2_scr, al_scr):
    dirn = pl.program_id(1)
    seg = pl.program_id(2)
    nb = q_ref.shape[0]
    nchunk = q_ref.shape[1] // CHUNK
    fwd = dirn == 0

    @pl.when(seg == 0)
    def _():
        s_scr[...] = jnp.zeros_like(s_scr)

    row = _iota((CHUNK, QUAD), 0)
    col = _mod(_iota((CHUNK, QUAD), 1), HEAD_DIM)
    later = jnp.where(fwd, row, col)
    earlier = jnp.where(fwd, col, row)
    incl = later >= earlier
    strict = later > earlier
    diag = row == col
    eye = diag.astype(BF16)
    unroll = min(nchunk, max(1, PHASE_A_PROBLEMS // (2 * nb)))
    assert nchunk % unroll == 0, (nchunk, unroll)

    def phase_a(it, carry):
        probs = []
        for u in range(unroll):
            c = it * unroll + u
            rows = pl.ds(pl.multiple_of(c * CHUNK, CHUNK), CHUNK)
            x_all = _dot(jnp.concatenate([cp_ref[bi, rows, :] for bi in range(nb)], axis=0), e2_ref[...])
            for bi in range(nb):
                x = x_all[bi * CHUNK:(bi + 1) * CHUNK]
                for qd in range(2):
                    lanes = slice(qd * QUAD, (qd + 1) * QUAD)
                    probs.append(dict(
                        c=c, slot=2 * bi + qd, gcol=x[:, qd * QUAD:(qd + 1) * QUAD],
                        beta=x[:, WIDTH + qd * QUAD:WIDTH + (qd + 1) * QUAD],
                        kq=k_ref[bi, rows, lanes], qq=q_ref[bi, rows, lanes], vq=v_ref[bi, rows, lanes]))
        for pr in probs:
            pr["kf"] = pr["kq"].astype(F32)
            pr["z"] = _dot_nt(jnp.concatenate([pr["kq"], pr["qq"], eye], axis=0),
                              _block_diag(pr["kf"]))
        for pr in probs:
            gcol, beta, z = pr["gcol"], pr["beta"], pr["z"]
            grow = jnp.sum(jnp.where(diag, gcol, 0.0), axis=0, keepdims=True)
            glast = jnp.where(fwd, gcol[CHUNK - 1:CHUNK, :], gcol[0:1, :])
            dm = jnp.exp(jnp.where(incl, gcol - grow, NEG_INF))
            eg = jnp.exp(gcol)
            a = jnp.where(strict, beta * z[0:CHUNK] * dm, 0.0)
            qkm = jnp.where(incl, z[CHUNK:2 * CHUNK] * dm, 0.0) * 0.125
            kdt = z[2 * CHUNK:3 * CHUNK] * jnp.exp(glast - grow)
            pr["vb"] = pr["vq"].astype(F32) * beta
            pr["kbg"] = pr["kf"] * (beta * eg)
            qdec = pr["qq"].astype(F32) * (eg * 0.125)
            c, slot = pr["c"], pr["slot"]
            l2_scr[slot, c] = jnp.concatenate([qkm, kdt], axis=0).astype(BF16)
            l1_scr[slot, c, CHUNK:2 * CHUNK, :] = qdec.astype(BF16)
            al_scr[slot, c] = jnp.broadcast_to(jnp.exp(glast), (8, QUAD))
            pr["a"] = a
        for pr in probs:
            pr["p"] = _dot(pr["a"].astype(BF16), _block_diag(pr["a"]))
            pr["n"] = -pr["a"]
        for _ in range(4):
            for pr in probs:
                n, p = pr["n"], pr["p"]
                zz = _dot(jnp.concatenate([n, p], axis=0).astype(BF16), _block_diag(p))
                pr["n"] = n + p + zz[0:CHUNK]
                pr["p"] = zz[CHUNK:2 * CHUNK]
        for pr in probs:
            n, p = pr["n"], pr["p"]
            pr["n"] = n + p + _dot(n.astype(BF16), _block_diag(p))
        for pr in probs:
            vb, kbg, c, slot = pr["vb"], pr["kbg"], pr["c"], pr["slot"]
            rhs = jnp.concatenate([_block_diag(vb), _block_diag(kbg)], axis=1)
            uw = _dot(pr["n"].astype(BF16), rhs)
            u_scr[slot, c] = vb + uw[:, 0:QUAD]
            l1_scr[slot, c, 0:CHUNK, :] = (kbg + uw[:, QUAD:2 * QUAD]).astype(BF16)
        return carry

    lax.fori_loop(0, nchunk // unroll, phase_a, 0)

    nslot = 2 * nb

    def phase_b(i, states):
        c = jnp.where(fwd, i, nchunk - 1 - i)
        r0 = pl.multiple_of(c * CHUNK, CHUNK)
        z1 = [_dot(l1_scr[sl, c], _block_diag(states[sl])) for sl in range(nslot)]
        vn = [u_scr[sl, c] - z1[sl][0:CHUNK] for sl in range(nslot)]
        z2 = [_dot(l2_scr[sl, c], _block_diag(vn[sl])) for sl in range(nslot)]
        new = []
        for sl in range(nslot):
            o = z1[sl][CHUNK:2 * CHUNK] + z2[sl][0:CHUNK]
            o_ref[sl // 2, pl.ds(r0, CHUNK), (sl % 2) * QUAD:(sl % 2 + 1) * QUAD] = o.astype(o_ref.dtype)
            new.append(al_scr[sl, c][0:1, :] * states[sl] + z2[sl][CHUNK:2 * CHUNK])
        return tuple(new)

    final = lax.fori_loop(0, nchunk, phase_b, tuple(s_scr[sl] for sl in range(nslot)))
    for sl in range(nslot):
        s_scr[sl] = final[sl]


def _dn_scan(q, k, v, cp, e2, nb=8, seg=256):
    b, s, _ = q.shape
    while b % nb:
        nb //= 2
    seg = min(seg, s)
    nseg = s // seg
    nchunk = seg // CHUNK

    def tok(i, d, j):
        return (i, jnp.where(d == 0, j, nseg - 1 - j), 0)

    return pl.pallas_call(
        _dn_scan_kernel,
        grid=(b // nb, 2, nseg),
        in_specs=[pl.BlockSpec((nb, seg, WIDTH), tok)] * 3
        + [pl.BlockSpec((nb, seg, LANES), tok),
           pl.BlockSpec((None, LANES, 2 * WIDTH), lambda i, d, j: (d, 0, 0))],
        out_specs=pl.BlockSpec((None, nb, seg, WIDTH),
                               lambda i, d, j: (d, i, jnp.where(d == 0, j, nseg - 1 - j), 0)),
        out_shape=jax.ShapeDtypeStruct((2, b, s, WIDTH), BF16),
        scratch_shapes=[
            pltpu.VMEM((2 * nb, CHUNK, QUAD), F32),
            pltpu.VMEM((2 * nb, nchunk, CHUNK, QUAD), F32),
            pltpu.VMEM((2 * nb, nchunk, 2 * CHUNK, QUAD), BF16),
            pltpu.VMEM((2 * nb, nchunk, 2 * CHUNK, QUAD), BF16),
            pltpu.VMEM((2 * nb, nchunk, 8, QUAD), F32),
        ],
        compiler_params=_cparams(("parallel", "arbitrary", "arbitrary")),
        name="dn_scan",
    )(q, k, v, cp, e2)


def _dn_expand_matrix():
    r = jnp.arange(LANES)[:, None]
    c = jnp.arange(2 * WIDTH)[None, :]
    piece, lane = r // 16, r % 16
    out = []
    for d in range(2):
        head = lane - 8 * d
        ok = (piece < 6) & (head >= 0) & (head < HEADS)
        ok = ok & ((c // WIDTH) == (piece // 3)) & (((c % WIDTH) // HEAD_DIM) == head)
        out.append(ok)
    return jnp.stack(out).astype(BF16)


QBLK = 2 * RADIUS
ATTN_ROWS = 1024


def _attn_kernel(q_ref, k_ref, kp_ref, kn_ref, v_ref, vp_ref, vn_ref, qw_ref, kw_ref, hm_ref, bias_ref,
                 o_ref, lse_ref, *, sub):
    for si in range(q_ref.shape[0]):
        _attn_tile(*(r.at[si] for r in (q_ref, k_ref, kp_ref, kn_ref, v_ref, vp_ref, vn_ref)),
                   qw_ref, kw_ref, hm_ref, bias_ref, o_ref.at[si], lse_ref.at[si], sub=sub)


def _attn_tile(q_ref, k_ref, kp_ref, kn_ref, v_ref, vp_ref, vn_ref, qw_ref, kw_ref, hm_ref, bias_ref,
               o_ref, lse_ref, *, sub):
    t = pl.program_id(1)
    tq = q_ref.shape[0]
    nqb = tq // QBLK
    low = _mod(_iota((1, WIDTH), 1), LANES) < HEAD_DIM
    hm = hm_ref[...]
    q = q_ref[...].astype(F32)
    qn = q * lax.rsqrt(_dot((q * q).astype(BF16), hm) + EPS) * (qw_ref[...] * (HEAD_DIM ** -0.5 * LOG2E))
    qn = qn.astype(BF16)
    zero = jnp.zeros((), BF16)
    one = jnp.ones((), BF16)
    qsel = (jnp.where(low, qn, zero), jnp.where(low, zero, qn))
    kx = jnp.concatenate([kp_ref[...], k_ref[...], kn_ref[...]], axis=0).astype(F32)
    kn = (kx * lax.rsqrt(_dot((kx * kx).astype(BF16), hm) + EPS) * kw_ref[...]).astype(BF16)
    vx = jnp.concatenate([vp_ref[...], v_ref[...], vn_ref[...]], axis=0)
    vsel = (jnp.where(low, vx, one), jnp.where(low, one, vx))
    c = _iota((QBLK, 2 * QBLK), 1)
    lane = _iota((1, LANES), 1)
    low_pair = lane < HEAD_DIM
    pairs = [slice((h // 2) * LANES, (h // 2 + 1) * LANES) for h in range(HEADS)]
    for qb in range(nqb):
        rows = slice(qb * QBLK, (qb + 1) * QBLK)
        krows = slice(qb * QBLK, qb * QBLK + 2 * QBLK)
        scores = [_dot_nt(qsel[h % 2][rows, pairs[h]], kn[krows, pairs[h]]) for h in range(HEADS)]
        edge = None
        if qb == 0 or qb == nqb - 1:
            kj = t * tq + (qb * QBLK - RADIUS) + c
            edge = jnp.where((kj >= 0) & (kj < sub), 0.0, NEG_INF)
        probs, maxes = [], []
        for h in range(HEADS):
            sc = scores[h] + bias_ref[h]
            if edge is not None:
                sc = sc + edge
            m = jnp.max(sc, axis=-1, keepdims=True)
            probs.append(jnp.exp2(sc - m).astype(BF16))
            maxes.append(m)
        outs = [_dot(probs[h], vsel[h % 2][krows, pairs[h]]) for h in range(HEADS)]
        lse_tile = jnp.zeros((QBLK, LANES), F32)
        for h0 in range(0, HEADS, 2):
            even, odd = outs[h0], outs[h0 + 1]
            num = jnp.where(low_pair, even, odd)
            den = pltpu.roll(jnp.where(low_pair, odd, even), HEAD_DIM, 1)
            o_ref[rows, pairs[h0]] = (num / den).astype(o_ref.dtype)
            lse_pair = jnp.where(low_pair, maxes[h0], maxes[h0 + 1]) * LN2 + jnp.log(den)
            keep = (lane == _lse_lane(h0)) | (lane == _lse_lane(h0 + 1))
            lse_tile = lse_tile + jnp.where(keep, lse_pair, 0.0)
        lse_ref[rows, :] = lse_tile


def _alibi_bias(dil, group):
    a = jnp.arange(QBLK)[:, None]
    c = jnp.arange(2 * QBLK)[None, :]
    rel = jnp.abs(a + RADIUS - c)
    slopes = 2.0 ** (-ALIBI_MAX * (group * HEADS + jnp.arange(1, HEADS + 1, dtype=F32)) / N_ATT_HEADS)
    bias = -(slopes * LOG2E)[:, None, None] * (rel * dil).astype(F32)[None]
    return jnp.where((rel <= RADIUS)[None], bias, NEG_INF)


def _attn(qkv, qw, kw, hm, *, dil, group):
    nseq, sub, _ = qkv.shape
    tq = min(ATTN_ROWS, sub)
    ns = max(1, ATTN_ROWS // sub)
    while nseq % ns:
        ns //= 2
    nt = sub // tq
    per = tq // RADIUS
    nblk = sub // RADIUS

    def main(lb):
        return pl.BlockSpec((ns, tq, WIDTH), lambda i, j: (i, j, lb))

    def prev(lb):
        return pl.BlockSpec((ns, RADIUS, WIDTH), lambda i, j: (i, jnp.maximum(j * per - 1, 0), lb))

    def nxt(lb):
        return pl.BlockSpec((ns, RADIUS, WIDTH),
                            lambda i, j: (i, jnp.minimum((j + 1) * per, nblk - 1), lb))

    wspec = pl.BlockSpec((1, WIDTH), lambda i, j: (0, 0))
    hmspec = pl.BlockSpec((WIDTH, WIDTH), lambda i, j: (0, 0), pipeline_mode=pl.Buffered(1))
    bspec = pl.BlockSpec((HEADS, QBLK, 2 * QBLK), lambda i, j: (0, 0, 0), pipeline_mode=pl.Buffered(1))
    return pl.pallas_call(
        functools.partial(_attn_kernel, sub=sub),
        grid=(nseq // ns, nt),
        in_specs=[main(0), main(1), prev(1), nxt(1), main(2), prev(2), nxt(2), wspec, wspec, hmspec, bspec],
        out_specs=[pl.BlockSpec((ns, tq, WIDTH), lambda i, j: (i, j, 0)),
                   pl.BlockSpec((ns, tq, LANES), lambda i, j: (i, j, 0))],
        out_shape=[jax.ShapeDtypeStruct((nseq, sub, WIDTH), BF16),
                   jax.ShapeDtypeStruct((nseq, sub, LANES), F32)],
        compiler_params=_cparams(("parallel", "parallel")),
        name=f"attn_g{group}",
    )(qkv, qkv, qkv, qkv, qkv, qkv, qkv, qw, kw, hm, _alibi_bias(dil, group))


def _merge_kernel(of_ref, ob_ref, z_ref, o0_ref, o1_ref, o2_ref, l0_ref, l1_ref, l2_ref,
                  gate_ref, x_ref, wdn_ref, wat_ref, wo_ref, dnw_ref, n2w_ref, hm_ref, pinv_ref, expand_ref,
                  x1_ref, h2_ref):
    tm = x_ref.shape[0]
    dm = x_ref.shape[1]
    sub = pinv_ref.shape[1]
    expand = expand_ref[...]
    blocks = [dict(rows=slice(t * sub, (t + 1) * sub), t=t) for t in range(tm // sub)]
    for bk in blocks:
        rows, t = bk["rows"], bk["t"]
        oa = of_ref[rows, :].astype(F32) + ob_ref[rows, :].astype(F32)
        bk["oa"] = oa
        bk["msq"] = _head_meansq(oa, hm_ref[...])
        outs = [o0_ref[rows, :].astype(F32)]
        lses = [l0_ref[rows, :]]
        for idx, (d, oref, lref) in enumerate(((4, o1_ref, l1_ref), (16, o2_ref, l2_ref))):
            pinv = pinv_ref[idx]
            run = sub // d
            part = slice(t * run, (t + 1) * run)
            outs.append(_dot(pinv, jnp.concatenate([oref[r, part, :] for r in range(d)], axis=0)))
            lflat = jnp.concatenate([lref[r, part, :] for r in range(d)], axis=0)
            lp = _dot(pinv, jnp.concatenate(_split3(lflat), axis=1))
            lses.append(lp[:, 0:LANES] + lp[:, LANES:2 * LANES] + lp[:, 2 * LANES:3 * LANES])
        bk["outs"], bk["lses"] = outs, lses
    for bk in blocks:
        rows = bk["rows"]
        z = z_ref[rows, :].astype(F32)
        gated = bk["oa"] * lax.rsqrt(bk["msq"] + EPS) * dnw_ref[...] * _silu(z)
        bk["ya"] = _dot(gated.astype(BF16), wdn_ref[...])
        lses = bk["lses"]
        m = jnp.maximum(jnp.maximum(lses[0], lses[1]), lses[2])
        es = [jnp.exp(l - m) for l in lses]
        den = es[0] + es[1] + es[2]
        ob = jnp.zeros((sub, WIDTH), F32)
        for e, o in zip(es, bk["outs"]):
            hi, mid, _ = _split3(e / den)
            ob = ob + _dot(jnp.concatenate([hi, mid], axis=1), expand) * o
        bk["ob"] = ob
    for bk in blocks:
        bk["yb"] = _dot(bk["ob"].astype(BF16), wat_ref[...])
    for bk in blocks:
        rows = bk["rows"]
        g = gate_ref[rows, :].astype(F32)
        mixed = _sigmoid(g[:, 0:dm]) * bk["ya"] + _sigmoid(g[:, dm:2 * dm]) * bk["yb"]
        x1 = x_ref[rows, :] + _dot(mixed.astype(BF16), wo_ref[...])
        x1_ref[rows, :] = x1
        ms = jnp.mean(x1 * x1, axis=-1, keepdims=True)
        h2_ref[rows, :] = (x1 * lax.rsqrt(ms + EPS) * n2w_ref[...]).astype(BF16)


def _merge(o_dn, z, attn_o, attn_l, gates, x, wdn, wat, wo, dnw, n2w, hm, tm=2 * PERM_TILE):
    b, s, dm = x.shape
    tok = lambda i, j: (i, j, 0)
    const = lambda i, j: (0, 0)
    pinv = jnp.stack([_perm_matrix(PERM_TILE, 4, inverse=True), _perm_matrix(PERM_TILE, 16, inverse=True)])
    in_specs = [
        pl.BlockSpec((None, None, tm, WIDTH), lambda i, j: (0, i, j, 0)),
        pl.BlockSpec((None, None, tm, WIDTH), lambda i, j: (1, i, j, 0)),
        pl.BlockSpec((None, tm, WIDTH), tok),
        pl.BlockSpec((None, tm, WIDTH), tok),
        pl.BlockSpec((None, 4, tm // 4, WIDTH), lambda i, j: (i, 0, j, 0)),
        pl.BlockSpec((None, 16, tm // 16, WIDTH), lambda i, j: (i, 0, j, 0)),
        pl.BlockSpec((None, tm, LANES), tok),
        pl.BlockSpec((None, 4, tm // 4, LANES), lambda i, j: (i, 0, j, 0)),
        pl.BlockSpec((None, 16, tm // 16, LANES), lambda i, j: (i, 0, j, 0)),
        pl.BlockSpec((None, tm, 2 * dm), tok),
        pl.BlockSpec((None, tm, dm), tok),
        pl.BlockSpec((WIDTH, dm), const),
        pl.BlockSpec((WIDTH, dm), const),
        pl.BlockSpec((dm, dm), const),
        pl.BlockSpec((1, WIDTH), const),
        pl.BlockSpec((1, dm), const),
        pl.BlockSpec((WIDTH, WIDTH), const, pipeline_mode=pl.Buffered(1)),
        pl.BlockSpec((2, PERM_TILE, PERM_TILE), lambda i, j: (0, 0, 0), pipeline_mode=pl.Buffered(1)),
        pl.BlockSpec((2 * LANES, WIDTH), const, pipeline_mode=pl.Buffered(1)),
    ]
    expand = _head_expand()
    return pl.pallas_call(
        _merge_kernel,
        grid=(b, s // tm),
        in_specs=in_specs,
        out_specs=[pl.BlockSpec((None, tm, dm), tok), pl.BlockSpec((None, tm, dm), tok)],
        out_shape=[jax.ShapeDtypeStruct((b, s, dm), F32), jax.ShapeDtypeStruct((b, s, dm), BF16)],
        compiler_params=_cparams(("parallel", "parallel")),
        name="merge",
    )(o_dn, o_dn, z, attn_o[0], attn_o[1], attn_o[2], attn_l[0], attn_l[1], attn_l[2],
      gates, x, wdn, wat, wo, dnw, n2w, hm, pinv, jnp.concatenate([expand, expand], axis=0))


def _ffn_kernel(h_ref, hp_ref, hn_ref, wup_ref, cw_ref, wd_ref, x1_ref, o_ref,
                ug_ref, uu_ref, *, tiles_per_seq, chunks):
    i = pl.program_id(0)
    tm = h_ref.shape[0]
    dff = wd_ref.shape[0]
    pos = _mod(i, tiles_per_seq)
    hp = hp_ref[...] * jnp.where(pos > 0, 1.0, 0.0).astype(BF16)
    hn = hn_ref[...] * jnp.where(pos < tiles_per_seq - 1, 1.0, 0.0).astype(BF16)
    lhs = jnp.concatenate([hp, h_ref[...], hn], axis=0)

    def conv(c0, c1, ue_ref):
        w = c1 - c0
        cw = cw_ref[:, c0:c1]
        return (cw[0:1] * ue_ref[BF16_ROWS - 1:BF16_ROWS - 1 + tm, 0:w]
                + cw[1:2] * ue_ref[BF16_ROWS:BF16_ROWS + tm, 0:w]
                + cw[2:3] * ue_ref[BF16_ROWS + 1:BF16_ROWS + 1 + tm, 0:w])

    for idx, (c0, c1) in enumerate(chunks):
        ug_ref[idx, :, 0:c1 - c0] = _dot(lhs, wup_ref[:, c0:c1])
        uu_ref[idx, :, 0:c1 - c0] = _dot(lhs, wup_ref[:, dff + c0:dff + c1])
    out = x1_ref[...]
    for idx, (c0, c1) in enumerate(chunks):
        act = _silu(conv(c0, c1, ug_ref.at[idx])) * conv(dff + c0, dff + c1, uu_ref.at[idx])
        out = out + _dot(act.astype(BF16), wd_ref[c0:c1, :])
    o_ref[...] = out


MXU_TILE = 256
FFN_CHUNK_TILES = 3


def _ffn(h2, x1, w_up, conv_w, w_down, seq, tm=512):
    n, dm = h2.shape
    dff = w_down.shape[0]
    split = FFN_CHUNK_TILES * MXU_TILE
    chunks = tuple((c0, min(c0 + split, dff)) for c0 in range(0, dff, split))
    per = tm // BF16_ROWS
    nblk = n // BF16_ROWS
    const = lambda i: (0, 0)

    def resident(shape):
        return pl.BlockSpec(shape, const, pipeline_mode=pl.Buffered(1))

    return pl.pallas_call(
        functools.partial(_ffn_kernel, tiles_per_seq=seq // tm, chunks=chunks),
        grid=(n // tm,),
        in_specs=[
            pl.BlockSpec((tm, dm), lambda i: (i, 0)),
            pl.BlockSpec((BF16_ROWS, dm), lambda i: (jnp.maximum(i * per - 1, 0), 0)),
            pl.BlockSpec((BF16_ROWS, dm), lambda i: (jnp.minimum((i + 1) * per, nblk - 1), 0)),
            resident(w_up.shape),
            resident(conv_w.shape),
            resident(w_down.shape),
            pl.BlockSpec((tm, dm), lambda i: (i, 0)),
        ],
        out_specs=pl.BlockSpec((tm, dm), lambda i: (i, 0)),
        out_shape=jax.ShapeDtypeStruct((n, dm), F32),
        scratch_shapes=[pltpu.VMEM((len(chunks), tm + 2 * BF16_ROWS, split), F32),
                        pltpu.VMEM((len(chunks), tm + 2 * BF16_ROWS, split), F32)],
        compiler_params=_cparams(("parallel",)),
        name="ffn",
    )(h2, h2, h2, w_up, conv_w, w_down, x1)


def _layer(x, norm1_w, w_in, dn_conv_w, dn_a_log, dn_dt_bias, dn_out_norm_w, attn_q_norm_w,
           attn_k_norm_w, w_dn_out, w_attn_out, w_o, norm2_w, w_ffn_up, ffn_conv_w, w_ffn_down):
    b, s, dm = x.shape
    n = b * s
    w3 = 3 * WIDTH
    assert s % (max(DILATIONS) * QBLK) == 0 and s % (2 * PERM_TILE) == 0, s
    assert w_in.shape == (dm, 4 * w3 + WIDTH + 32 + 2 * dm), w_in.shape
    c_z = w3
    c_ab = c_z + WIDTH
    c_q = c_ab + 32
    c_k = c_q + w3
    c_v = c_k + w3
    c_gate = c_v + w3
    wb = w_in.astype(BF16)
    w_a = jnp.pad(wb[:, 0:c_q], ((0, 0), (0, LANES - 32)))
    w_grp = [jnp.concatenate([wb[:, c0 + g * WIDTH:c0 + (g + 1) * WIDTH] for c0 in (c_q, c_k, c_v)], axis=1)
             for g in range(3)]
    w_gate = wb[:, c_gate:c_gate + 2 * dm]

    hm = _head_mean_matrix()

    q, k, v, z, gates, ab, qkv0, qkv1, qkv2 = _in_proj(x, norm1_w.reshape(1, dm), w_a, w_gate, w_grp,
                                                       dn_conv_w, hm)

    cp = _dn_gates(ab, dn_a_log.reshape(1, 16), dn_dt_bias.reshape(1, 16))
    o_dn = _dn_scan(q, k, v, cp, _dn_expand_matrix())

    qw = jnp.tile(attn_q_norm_w, HEADS).reshape(1, WIDTH)
    kw = jnp.tile(attn_k_norm_w, HEADS).reshape(1, WIDTH)
    attn_o, attn_l = [], []
    for g, (d, qkv) in enumerate(zip(DILATIONS, (qkv0, qkv1, qkv2))):
        o, l = _attn(qkv.reshape(b * d, s // d, w3), qw, kw, hm, dil=d, group=g)
        shape = (b, s) if d == 1 else (b, d, s // d)
        attn_o.append(o.reshape(shape + (WIDTH,)))
        attn_l.append(l.reshape(shape + (LANES,)))

    x1, h2 = _merge(o_dn, z, attn_o, attn_l, gates, x, w_dn_out.astype(BF16), w_attn_out.astype(BF16),
                    w_o.astype(BF16), jnp.tile(dn_out_norm_w, HEADS).reshape(1, WIDTH),
                    norm2_w.reshape(1, dm), hm)
    out = _ffn(h2.reshape(n, dm), x1.reshape(n, dm), w_ffn_up.astype(BF16), ffn_conv_w,
               w_ffn_down.astype(BF16), s)
    return out.reshape(b, s, dm)


def kernel(x, norm1_w, w_in, dn_conv_w, dn_a_log, dn_dt_bias, dn_out_norm_w, attn_q_norm_w, attn_k_norm_w, w_dn_out, w_attn_out, w_o, norm2_w, w_ffn_up, ffn_conv_w, w_ffn_down):
    for layer in range(norm1_w.shape[0]):
        x = _layer(x, norm1_w[layer], w_in[layer], dn_conv_w[layer], dn_a_log[layer], dn_dt_bias[layer],
                   dn_out_norm_w[layer], attn_q_norm_w[layer], attn_k_norm_w[layer], w_dn_out[layer],
                   w_attn_out[layer], w_o[layer], norm2_w[layer], w_ffn_up[layer], ffn_conv_w[layer],
                   w_ffn_down[layer])
    return x
```
